```python
import jax
import jax.numpy as jnp
from jax import lax
import numpy as np

D_MODEL = 2048
BATCH = 8
SEQ = 2048
DEPTH = 1
DEC_BATCH = 32
DEC_SEQ = 64
PAST_LEN = 2048

CHUNK = 64
N_HEADS = 16
N_KV_HEADS = 4
HEAD_DIM = 64
GROUP = N_HEADS // N_KV_HEADS
WINDOW = 128
WINDOW_CHUNKS = WINDOW // CHUNK
ROPE_THETA = 10000.0
N_REC_HEADS = 8
REC_DK = 128
REC_DV = 128
N_EXPERTS = 32
TOP_K = 4
D_FF = 2048
SWIGLU_LIMIT = 7.0
SWIGLU_ALPHA = 1.702
MOE_BLOCK = 128
EPS = 1e-6
ATTN_Q_W = N_HEADS * HEAD_DIM
ATTN_KV_W = N_KV_HEADS * HEAD_DIM
REC_K_W = N_REC_HEADS * REC_DK
REC_V_W = N_REC_HEADS * REC_DV
IN_WIDTHS = (ATTN_Q_W, ATTN_KV_W, ATTN_KV_W, REC_K_W, REC_K_W, REC_V_W, REC_V_W, D_MODEL, D_MODEL)
IN_W = ATTN_Q_W + 2 * ATTN_KV_W + 2 * REC_K_W + 2 * REC_V_W + 2 * D_MODEL

kernel_name = 'hybrid_swa_hgrn2_moe_stream_step'


def rms_norm(x, w):
    xf = x.astype(jnp.float32)
    y = xf * lax.rsqrt(jnp.mean(xf * xf, axis=-1, keepdims=True) + EPS)
    return (y * w.astype(jnp.float32)).astype(x.dtype)


def split_cols(a, widths):
    parts, off = [], 0
    for w in widths:
        parts.append(a[..., off:off + w])
        off += w
    return parts


def rope(x, pos):
    half = HEAD_DIM // 2
    inv_freq = ROPE_THETA ** (-jnp.arange(half, dtype=jnp.float32) / half)
    ang = pos.astype(jnp.float32)[:, None] * inv_freq[None, :]
    cos = jnp.cos(ang)[None, :, None, :]
    sin = jnp.sin(ang)[None, :, None, :]
    xf = x.astype(jnp.float32)
    x1, x2 = xf[..., :half], xf[..., half:]
    return jnp.concatenate([x1 * cos - x2 * sin, x2 * cos + x1 * sin], axis=-1).astype(x.dtype)


def attend_with_sinks(q, k, v, key_valid, sinks):
    s = jnp.einsum('bnqkgd,bnskd->bnkgqs', q, k).astype(jnp.float32) * (HEAD_DIM ** -0.5)
    s = jnp.where(key_valid[None, :, None, None, None, :], s, -jnp.inf)
    sink = sinks.astype(jnp.float32).reshape(1, 1, N_KV_HEADS, GROUP, 1, 1)
    m = jnp.maximum(jnp.max(s, axis=-1, keepdims=True), sink)
    p = jnp.exp(s - m)
    p = p / (jnp.sum(p, axis=-1, keepdims=True) + jnp.exp(sink - m))
    return jnp.einsum('bnkgqs,bnskd->bnqkgd', p.astype(v.dtype), v)


def window_attention_prompt(q, k, v, sinks):
    b, t = q.shape[:2]
    n = t // CHUNK
    qb = q.reshape(b, n, CHUNK, N_KV_HEADS, GROUP, HEAD_DIM)
    pad = ((0, 0), (WINDOW_CHUNKS * CHUNK, 0), (0, 0), (0, 0))
    kc = jnp.pad(k, pad).reshape(b, n + WINDOW_CHUNKS, CHUNK, N_KV_HEADS, HEAD_DIM)
    vc = jnp.pad(v, pad).reshape(b, n + WINDOW_CHUNKS, CHUNK, N_KV_HEADS, HEAD_DIM)
    kb = jnp.concatenate([kc[:, j:j + n] for j in range(WINDOW_CHUNKS + 1)], axis=2)
    vb = jnp.concatenate([vc[:, j:j + n] for j in range(WINDOW_CHUNKS + 1)], axis=2)
    key_chunk = (jnp.arange(n)[:, None] - WINDOW_CHUNKS
                 + jnp.arange((WINDOW_CHUNKS + 1) * CHUNK)[None, :] // CHUNK)
    o = attend_with_sinks(qb, kb, vb, key_chunk >= 0, sinks)
    return o.reshape(b, t, ATTN_Q_W)


def window_attention_sample(q, k_all, v_all, sinks):
    b, t = q.shape[:2]
    qb = q.reshape(b, 1, t, N_KV_HEADS, GROUP, HEAD_DIM)
    valid = jnp.ones((1, k_all.shape[1]), dtype=bool)
    o = attend_with_sinks(qb, k_all[:, None], v_all[:, None], valid, sinks)
    return o.reshape(b, t, ATTN_Q_W)


def hgrn2_recurrence(q, log_f, k, v, s0, chunk):
    b, t = q.shape[:2]
    n = t // chunk

    def to_chunks(a):
        a = a.astype(jnp.float32).reshape(b, n, chunk, N_REC_HEADS, a.shape[-1])
        return a.transpose(1, 0, 3, 2, 4)

    causal = jnp.tril(jnp.ones((chunk, chunk), dtype=bool))[:, :, None]

    def step(s, xs):
        qi, gi, ki, vi = xs
        cum = jnp.cumsum(gi, axis=2)
        inter = jnp.einsum('bhtc,bhcv->bhtv', qi * jnp.exp(cum), s)
        decay = jnp.exp(jnp.where(causal, cum[:, :, :, None, :] - cum[:, :, None, :, :], -jnp.inf))
        scores = jnp.einsum('bhtc,bhsc,bhtsc->bhts', qi, ki, decay)
        intra = jnp.einsum('bhts,bhsv->bhtv', scores, vi)
        last = cum[:, :, -1:, :]
        s_new = (jnp.exp(last[:, :, 0, :])[..., None] * s
                 + jnp.einsum('bhsc,bhsv->bhcv', ki * jnp.exp(last - cum), vi))
        return s_new, inter + intra

    s_fin, o = lax.scan(step, s0.astype(jnp.float32),
                        (to_chunks(q), to_chunks(log_f), to_chunks(k), to_chunks(v)))
    o = o.transpose(1, 0, 3, 2, 4).reshape(b, t, N_REC_HEADS, REC_DV)
    return o, s_fin


def moe(h, router_w, router_b, w_gate_up, b_gate_up, w_down, b_down):
    n_tok = h.shape[0]
    n_assign = n_tok * TOP_K
    n_blocks = -(-n_assign // MOE_BLOCK) + N_EXPERTS
    logits = (h @ router_w).astype(jnp.float32) + router_b.astype(jnp.float32)
    top_logit, top_e = lax.top_k(logits, TOP_K)
    gate = jax.nn.softmax(top_logit, axis=-1)
    flat_e = top_e.reshape(n_assign)
    order = jnp.argsort(flat_e)
    sorted_e = flat_e[order]
    counts = jnp.bincount(flat_e, length=N_EXPERTS)
    starts = jnp.cumsum(counts) - counts
    padded = (counts + MOE_BLOCK - 1) // MOE_BLOCK * MOE_BLOCK
    pad_end = jnp.cumsum(padded)
    pad_start = pad_end - padded
    slot = pad_start[sorted_e] + (jnp.arange(n_assign) - starts[sorted_e])
    slot_tok = jnp.full((n_blocks * MOE_BLOCK,), n_tok, jnp.int32).at[slot].set((order // TOP_K).astype(jnp.int32))
    slot_w = jnp.zeros((n_blocks * MOE_BLOCK,), jnp.float32).at[slot].set(gate.reshape(n_assign)[order])
    block_e = jnp.minimum(jnp.searchsorted(pad_end, jnp.arange(n_blocks) * MOE_BLOCK, side='right'),
                          N_EXPERTS - 1)
    h_pad = jnp.concatenate([h, jnp.zeros((1, h.shape[1]), h.dtype)], axis=0)

    def run_block(args):
        tok, e = args
        xb = h_pad[tok]
        gu = xb @ w_gate_up[e] + b_gate_up[e]
        g = jnp.minimum(gu[:, :D_FF], SWIGLU_LIMIT)
        u = jnp.clip(gu[:, D_FF:], -SWIGLU_LIMIT, SWIGLU_LIMIT)
        glu = g * jax.nn.sigmoid(SWIGLU_ALPHA * g)
        return ((u + 1.0) * glu) @ w_down[e] + b_down[e]

    out = lax.map(run_block, (slot_tok.reshape(n_blocks, MOE_BLOCK), block_e))
    out = out.reshape(n_blocks * MOE_BLOCK, -1) * slot_w[:, None]
    y = jax.ops.segment_sum(out, slot_tok, num_segments=n_tok + 1)[:n_tok]
    return y.astype(h.dtype)


def trunk_layer(x, pos, cache_k, cache_v, s0, rec_chunk, lb, norm_mix_w, w_in, q_norm_w, k_norm_w,
                attn_sinks, rec_norm_w, w_attn_branch, w_rec_branch, w_out, norm_ffn_w, router_w,
                router_b, w_gate_up, b_gate_up, w_down, b_down):
    b, t, _ = x.shape
    h = rms_norm(x, norm_mix_w)
    aq, ak, av, rq, rf, rv, rg, ga, gr = split_cols(h @ w_in, IN_WIDTHS)
    q = rope(rms_norm(aq.reshape(b, t, N_HEADS, HEAD_DIM), q_norm_w), pos)
    k = rope(rms_norm(ak.reshape(b, t, N_KV_HEADS, HEAD_DIM), k_norm_w), pos)
    v = av.reshape(b, t, N_KV_HEADS, HEAD_DIM)
    if cache_k is None:
        o_attn = window_attention_prompt(q, k, v, attn_sinks)
        k_all, v_all = k, v
    else:
        k_all = jnp.concatenate([cache_k.astype(k.dtype), k], axis=1)
        v_all = jnp.concatenate([cache_v.astype(v.dtype), v], axis=1)
        o_attn = window_attention_sample(q, k_all, v_all, attn_sinks)
    k_win, v_win = k_all[:, -WINDOW:], v_all[:, -WINDOW:]
    rq = jax.nn.silu(rq.reshape(b, t, N_REC_HEADS, REC_DK))
    lb = lb.reshape(N_REC_HEADS, REC_DK)
    log_f = jnp.logaddexp(jnp.log(lb), jnp.log1p(-lb)
                          + jax.nn.log_sigmoid(rf.reshape(b, t, N_REC_HEADS, REC_DK).astype(jnp.float32)))
    rk = -jnp.expm1(log_f)
    o_rec, s_fin = hgrn2_recurrence(rq, log_f, rk, rv.reshape(b, t, N_REC_HEADS, REC_DV), s0, rec_chunk)
    o_rec = rms_norm(o_rec.astype(x.dtype), rec_norm_w) * jax.nn.silu(rg.reshape(b, t, N_REC_HEADS, REC_DV))
    o_rec = o_rec.reshape(b, t, REC_V_W)
    merged = (jax.nn.sigmoid(ga) * (o_attn @ w_attn_branch)
              + jax.nn.sigmoid(gr) * (o_rec @ w_rec_branch))
    x = x + merged @ w_out
    h2 = rms_norm(x, norm_ffn_w).reshape(b * t, D_MODEL)
    x = x + moe(h2, router_w, router_b, w_gate_up, b_gate_up, w_down, b_down).reshape(b, t, D_MODEL)
    return x, k_win, v_win, s_fin


def setup_inputs(seed: int = 0) -> dict:
    key = jax.random.key(seed)
    ks = jax.random.split(key, 24)
    f32 = jnp.float32

    def nrm(k, shape, scale):
        return jax.random.normal(k, shape, f32) * scale

    return {
        'x_prompt': nrm(ks[0], (BATCH, SEQ, D_MODEL), 1.0),
        'x_sample': nrm(ks[1], (DEC_BATCH, DEC_SEQ, D_MODEL), 1.0),
        'cache_k': nrm(ks[2], (DEPTH, DEC_BATCH, WINDOW, N_KV_HEADS, HEAD_DIM), 1.0),
        'cache_v': nrm(ks[3], (DEPTH, DEC_BATCH, WINDOW, N_KV_HEADS, HEAD_DIM), 1.0),
        'state_rec': nrm(ks[4], (DEPTH, DEC_BATCH, N_REC_HEADS, REC_DK, REC_DV), 0.5),
        'norm_mix_w': 1.0 + nrm(ks[5], (DEPTH, D_MODEL), 0.02),
        'w_in': nrm(ks[6], (DEPTH, D_MODEL, IN_W), D_MODEL ** -0.5),
        'q_norm_w': 1.0 + nrm(ks[7], (DEPTH, HEAD_DIM), 0.02),
        'k_norm_w': 1.0 + nrm(ks[8], (DEPTH, HEAD_DIM), 0.02),
        'attn_sinks': nrm(ks[9], (DEPTH, N_HEADS), 1.0),
        'rec_lb_logits': nrm(ks[10], (DEPTH + 1, REC_K_W), 1.0),
        'rec_norm_w': 1.0 + nrm(ks[11], (DEPTH, REC_DV), 0.02),
        'w_attn_branch': nrm(ks[12], (DEPTH, ATTN_Q_W, D_MODEL), ATTN_Q_W ** -0.5),
        'w_rec_branch': nrm(ks[13], (DEPTH, REC_V_W, D_MODEL), REC_V_W ** -0.5),
        'w_out': nrm(ks[14], (DEPTH, D_MODEL, D_MODEL), D_MODEL ** -0.5),
        'norm_ffn_w': 1.0 + nrm(ks[15], (DEPTH, D_MODEL), 0.02),
        'router_w': nrm(ks[16], (DEPTH, D_MODEL, N_EXPERTS), D_MODEL ** -0.5),
        'router_b': nrm(ks[17], (DEPTH, N_EXPERTS), 0.01),
        'w_gate_up': nrm(ks[18], (DEPTH, N_EXPERTS, D_MODEL, 2 * D_FF), D_MODEL ** -0.5),
        'b_gate_up': nrm(ks[19], (DEPTH, N_EXPERTS, 2 * D_FF), 0.02),
        'w_down': nrm(ks[20], (DEPTH, N_EXPERTS, D_FF, D_MODEL), D_FF ** -0.5),
        'b_down': nrm(ks[21], (DEPTH, N_EXPERTS, D_MODEL), 0.02),
    }


def reference(x_prompt, x_sample, cache_k, cache_v, state_rec, norm_mix_w, w_in, q_norm_w, k_norm_w,
              attn_sinks, rec_lb_logits, rec_norm_w, w_attn_branch, w_rec_branch, w_out, norm_ffn_w,
              router_w, router_b, w_gate_up, b_gate_up, w_down, b_down):
    lb_table = jnp.cumsum(jax.nn.softmax(rec_lb_logits.astype(jnp.float32), axis=0), axis=0)
    n_p = x_prompt.shape[1]
    n_s = x_sample.shape[1]
    pos_p = jnp.arange(n_p, dtype=jnp.int32)
    pos_s = PAST_LEN + jnp.arange(n_s, dtype=jnp.int32)
    yp, ys = x_prompt, x_sample
    kp_l, vp_l, sp_l, ks_l, vs_l, ss_l = [], [], [], [], [], []
    for l in range(DEPTH):
        weights = (norm_mix_w[l], w_in[l], q_norm_w[l], k_norm_w[l], attn_sinks[l], rec_norm_w[l],
                   w_attn_branch[l], w_rec_branch[l], w_out[l], norm_ffn_w[l], router_w[l], router_b[l],
                   w_gate_up[l], b_gate_up[l], w_down[l], b_down[l])
        s0 = jnp.zeros((x_prompt.shape[0], N_REC_HEADS, REC_DK, REC_DV), jnp.float32)
        yp, kp, vp, sp = trunk_layer(yp, pos_p, None, None, s0, CHUNK, lb_table[l], *weights)
        ys, kss, vss, sss = trunk_layer(ys, pos_s, cache_k[l], cache_v[l], state_rec[l], n_s,
                                        lb_table[l], *weights)
        kp_l.append(kp)
        vp_l.append(vp)
        sp_l.append(sp)
        ks_l.append(kss.astype(cache_k.dtype))
        vs_l.append(vss.astype(cache_v.dtype))
        ss_l.append(sss.astype(state_rec.dtype))
    return (yp, ys, jnp.stack(kp_l), jnp.stack(vp_l), jnp.stack(sp_l),
            jnp.stack(ks_l), jnp.stack(vs_l), jnp.stack(ss_l))
```

```python
import functools

import jax
import jax.numpy as jnp
from jax import lax
from jax.experimental import pallas as pl
from jax.experimental.pallas import tpu as pltpu

F32 = jnp.float32
BF16 = jnp.bfloat16
I32 = jnp.int32

CHUNK = 64
N_HEADS = 16
N_KV_HEADS = 4
HEAD_DIM = 64
WINDOW = 128
ROPE_THETA = 10000.0
N_REC_HEADS = 8
REC_D = 128
N_EXPERTS = 32
TOP_K = 4
SWIGLU_LIMIT = 7.0
SWIGLU_ALPHA = 1.702
EPS = 1e-6
PAST_LEN = 2048
LANES = 128
SUB = 16

VMEM_LIMIT = 56 * 1024 * 1024

Q_W = N_HEADS * HEAD_DIM
KV_W = N_KV_HEADS * HEAD_DIM
REC_W = N_REC_HEADS * REC_D


def _cparams(sem):
    return pltpu.CompilerParams(dimension_semantics=sem, vmem_limit_bytes=VMEM_LIMIT)


def _sigmoid(x):
    return 1.0 / (1.0 + jnp.exp(-x))


def _split2(x):
    hi = x.astype(BF16)
    lo = (x - hi.astype(F32)).astype(BF16)
    return hi, lo


def _dot(a, b):
    return jnp.dot(a, b, preferred_element_type=F32)


def _dot_nt(a, b):
    return lax.dot_general(a, b, (((1,), (1,)), ((), ())), preferred_element_type=F32)


def _in_proj_kernel(x_ref, nw_ref, w_ref, o_ref, h_scr):
    @pl.when(pl.program_id(1) == 0)
    def _():
        x = x_ref[...]
        ms = jnp.mean(x * x, axis=-1, keepdims=True)
        h_scr[...] = (x * lax.rsqrt(ms + EPS) * nw_ref[...]).astype(BF16)

    o_ref[...] = _dot(h_scr[...], w_ref[...])


def _in_proj(x, norm_w, w, tm, tn):
    t, d = x.shape
    n = w.shape[1]
    return pl.pallas_call(
        _in_proj_kernel,
        grid=(t // tm, n // tn),
        in_specs=[pl.BlockSpec((tm, d), lambda i, j: (i, 0)),
                  pl.BlockSpec((1, d), lambda i, j: (0, 0)),
                  pl.BlockSpec((d, tn), lambda i, j: (0, j))],
        out_specs=pl.BlockSpec((tm, tn), lambda i, j: (i, j)),
        out_shape=jax.ShapeDtypeStruct((t, n), F32),
        scratch_shapes=[pltpu.VMEM((tm, d), BF16)],
        compiler_params=_cparams(("parallel", "arbitrary")),
        name="in_proj",
    )(x, norm_w, w)


def _norm_rope(x, nw, cos, sin, bd):
    hi, lo = _split2(x * x)
    ss = _dot(hi, bd) + _dot(lo, bd)
    y = x * lax.rsqrt(ss * (1.0 / HEAD_DIM) + EPS) * nw
    lane = lax.broadcasted_iota(I32, y.shape, 1)
    first_half = (lane % HEAD_DIM) < (HEAD_DIM // 2)
    rot = jnp.where(first_half, pltpu.roll(y, LANES - HEAD_DIM // 2, 1), pltpu.roll(y, HEAD_DIM // 2, 1))
    return y * cos + rot * sin


def _qk_prep_kernel(q_ref, k_ref, cos_ref, sin_ref, qw_ref, kw_ref, qo_ref, ko_ref):
    r = lax.broadcasted_iota(I32, (LANES, LANES), 0) // HEAD_DIM
    c = lax.broadcasted_iota(I32, (LANES, LANES), 1) // HEAD_DIM
    bd = jnp.where(r == c, 1.0, 0.0).astype(BF16)
    cos = cos_ref[...]
    sin = sin_ref[...]
    scale = HEAD_DIM ** -0.5
    for g in range(Q_W // LANES):
        sl = slice(g * LANES, (g + 1) * LANES)
        qo_ref[:, sl] = (_norm_rope(q_ref[:, sl], qw_ref[...], cos, sin, bd) * scale).astype(BF16)
    for g in range(KV_W // LANES):
        sl = slice(g * LANES, (g + 1) * LANES)
        ko_ref[:, sl] = _norm_rope(k_ref[:, sl], kw_ref[...], cos, sin, bd)


def _qk_prep(proj, cos, sin, qw, kw, q_blk, k_blk, tr):
    t = proj.shape[0]
    return pl.pallas_call(
        _qk_prep_kernel,
        grid=(t // tr,),
        in_specs=[pl.BlockSpec((tr, Q_W), lambda i: (i, q_blk)),
                  pl.BlockSpec((tr, KV_W), lambda i: (i, k_blk)),
                  pl.BlockSpec((tr, LANES), lambda i: (i, 0)),
                  pl.BlockSpec((tr, LANES), lambda i: (i, 0)),
                  pl.BlockSpec((1, LANES), lambda i: (0, 0)),
                  pl.BlockSpec((1, LANES), lambda i: (0, 0))],
        out_specs=[pl.BlockSpec((tr, Q_W), lambda i: (i, 0)),
                   pl.BlockSpec((tr, KV_W), lambda i: (i, 0))],
        out_shape=[jax.ShapeDtypeStruct((t, Q_W), BF16), jax.ShapeDtypeStruct((t, KV_W), F32)],
        compiler_params=_cparams(("parallel",)),
        name="qk_prep",
    )(proj, proj, cos, sin, qw, kw)


def _attn_kernel(sink_ref, q_ref, k0_ref, k1_ref, k2_ref, v0_ref, v1_ref, v2_ref, o_ref, *, masked):
    k_all = jnp.concatenate([k0_ref[...], k1_ref[...], k2_ref[...]], axis=0)
    v_all = jnp.concatenate([v0_ref[...], v1_ref[...], v2_ref[...]], axis=0)
    n_keys = 3 * CHUNK
    lane = lax.broadcasted_iota(I32, (n_keys, LANES), 1)
    low = lane < HEAD_DIM
    row = lax.broadcasted_iota(I32, (2 * CHUNK, 1), 0)
    if masked:
        c = pl.program_id(1)
        key_chunk = lax.broadcasted_iota(I32, (2 * CHUNK, n_keys), 1) // CHUNK
        valid = (key_chunk + c) >= 2
    for j in range(N_KV_HEADS):
        pair = slice((j // 2) * LANES, (j // 2 + 1) * LANES)
        kp = k_all[:, pair]
        vp = v_all[:, pair]
        kr = pltpu.roll(kp, HEAD_DIM, 1)
        vr = pltpu.roll(vp, HEAD_DIM, 1)
        if j % 2 == 0:
            k_low, k_high, v_low, v_high = kp, kr, vp, vr
        else:
            k_low, k_high, v_low, v_high = kr, kp, vr, vp
        q2 = jnp.concatenate([q_ref[:, (2 * j) * LANES:(2 * j + 1) * LANES],
                              q_ref[:, (2 * j + 1) * LANES:(2 * j + 2) * LANES]], axis=0)
        o = jnp.zeros((2 * CHUNK, LANES), F32)
        for half, (kk, vv) in enumerate(((k_low, v_low), (k_high, v_high))):
            keep = low if half == 0 else jnp.logical_not(low)
            kz = jnp.where(keep, kk, 0.0).astype(BF16)
            vz = jnp.where(keep, vv, 0.0).astype(BF16)
            s = _dot_nt(q2, kz)
            if masked:
                s = jnp.where(valid, s, -jnp.inf)
            sink = jnp.where(row < CHUNK, sink_ref[4 * j + half], sink_ref[4 * j + 2 + half])
            m = jnp.maximum(jnp.max(s, axis=-1, keepdims=True), sink)
            p = jnp.exp(s - m)
            p = p / (jnp.sum(p, axis=-1, keepdims=True) + jnp.exp(sink - m))
            o = o + _dot(p.astype(BF16), vz)
        o_ref[:, (2 * j) * LANES:(2 * j + 1) * LANES] = o[:CHUNK].astype(BF16)
        o_ref[:, (2 * j + 1) * LANES:(2 * j + 2) * LANES] = o[CHUNK:].astype(BF16)


def _attention(sinks, qr, k_srcs, v_srcs, grid, q_map, o_map, n_rows, kv_maps, masked):
    def spec(col, fn):
        return pl.BlockSpec((CHUNK, KV_W), lambda *g: (fn(*g), col))

    in_specs = [pl.BlockSpec(memory_space=pltpu.SMEM),
                pl.BlockSpec((CHUNK, Q_W), lambda *g: (q_map(*g), 0))]
    in_specs += [spec(col, fn) for (_, col), fn in zip(k_srcs, kv_maps)]
    in_specs += [spec(col, fn) for (_, col), fn in zip(v_srcs, kv_maps)]
    return pl.pallas_call(
        functools.partial(_attn_kernel, masked=masked),
        grid=grid,
        in_specs=in_specs,
        out_specs=pl.BlockSpec((CHUNK, Q_W), lambda *g: (o_map(*g), 0)),
        out_shape=jax.ShapeDtypeStruct((n_rows, Q_W), BF16),
        compiler_params=_cparams(("parallel",) * len(grid)),
        name="attn_prompt" if masked else "attn_sample",
    )(sinks, qr, *[a for a, _ in k_srcs], *[a for a, _ in v_srcs])


def _hgrn2_kernel(rq_ref, rf_ref, rv_ref, rg_ref, lb_ref, nw_ref, s0_ref, o_ref, sfin_ref,
                  st_scr, q_scr, k_scr, v_scr, c_scr, *, n_chunks):
    t = pl.program_id(2)

    @pl.when(t == 0)
    def _():
        st_scr[...] = s0_ref[0, 0].T

    one_m_lb = 1.0 - lb_ref[...]
    r64 = lax.broadcasted_iota(I32, (CHUNK, CHUNK), 0)
    c64 = lax.broadcasted_iota(I32, (CHUNK, CHUNK), 1)
    tri = jnp.where(r64 >= c64, 1.0, 0.0).astype(BF16)
    ones = jnp.ones((LANES, LANES), BF16)
    rsub = lax.broadcasted_iota(I32, (SUB, LANES), 0)
    n_sub = CHUNK // SUB

    def chunk(ci, carry):
        r0 = pl.multiple_of(ci * CHUNK, CHUNK)
        rq = rq_ref[pl.ds(r0, CHUNK), :]
        q = rq * _sigmoid(rq)
        k = one_m_lb * _sigmoid(-rf_ref[pl.ds(r0, CHUNK), :])
        g = jnp.log1p(-k)
        v = rv_ref[pl.ds(r0, CHUNK), :]
        g1 = g.astype(BF16)
        e1 = g - g1.astype(F32)
        g2 = e1.astype(BF16)
        g3 = (e1 - g2.astype(F32)).astype(BF16)
        cum = _dot(tri, g1) + _dot(tri, g2) + _dot(tri, g3)
        q_scr[...] = q
        k_scr[...] = k
        v_scr[...] = v
        c_scr[...] = cum
        st = st_scr[...]
        inter = _dot_nt((q * jnp.exp(cum)).astype(BF16), st.astype(BF16))
        outs = []
        for i in range(n_sub):
            lo = i * SUB
            q_i = q_scr[lo:lo + SUB, :]
            c_i = c_scr[lo:lo + SUB, :]
            acc = inter[lo:lo + SUB, :]
            if i > 0:
                ref = c_scr[lo - 1:lo, :]
                qd = (q_i * jnp.exp(c_i - ref)).astype(BF16)
                kd = (k_scr[0:lo, :] * jnp.exp(ref - c_scr[0:lo, :])).astype(BF16)
                s = _dot_nt(qd, kd)
                acc = acc + _dot(s.astype(BF16), v_scr[0:lo, :].astype(BF16))
            parts = []
            for s_ in range(SUB):
                arg = jnp.where(rsub >= s_, c_i - c_scr[lo + s_:lo + s_ + 1, :], -jnp.inf)
                parts.append(q_i * k_scr[lo + s_:lo + s_ + 1, :] * jnp.exp(arg))
            dsum = _dot(jnp.concatenate(parts, axis=0).astype(BF16), ones)
            for s_ in range(SUB):
                acc = acc + dsum[s_ * SUB:(s_ + 1) * SUB, :] * v_scr[lo + s_:lo + s_ + 1, :]
            outs.append(acc)
        o = jnp.concatenate(outs, axis=0)
        last = c_scr[CHUNK - 1:CHUNK, :]
        kdec = (k * jnp.exp(last - cum)).astype(BF16)
        st_scr[...] = st * jnp.exp(last) + _dot(v.T.astype(BF16), kdec)
        ms = jnp.mean(o * o, axis=-1, keepdims=True)
        rg = rg_ref[pl.ds(r0, CHUNK), :]
        o_ref[pl.ds(r0, CHUNK), :] = (o * lax.rsqrt(ms + EPS) * nw_ref[...] * (rg * _sigmoid(rg))).astype(BF16)
        return carry

    lax.fori_loop(0, n_chunks, chunk, 0)

    @pl.when(t == pl.num_programs(2) - 1)
    def _():
        sfin_ref[0, 0] = st_scr[...].T


def _hgrn2(proj, lb, norm_w, s0, row_blk0, n_seq, seq_len, tb, col_blks, out_rows):
    nt = seq_len // tb
    cq, cf, cv, cg = col_blks

    def in_spec(cb):
        return pl.BlockSpec((tb, REC_D), lambda b, h, t: (row_blk0 + b * nt + t, cb + h))

    return pl.pallas_call(
        functools.partial(_hgrn2_kernel, n_chunks=tb // CHUNK),
        grid=(n_seq, N_REC_HEADS, nt),
        in_specs=[in_spec(cq), in_spec(cf), in_spec(cv), in_spec(cg),
                  pl.BlockSpec((1, REC_D), lambda b, h, t: (0, h)),
                  pl.BlockSpec((1, REC_D), lambda b, h, t: (0, 0)),
                  pl.BlockSpec((1, 1, REC_D, REC_D), lambda b, h, t: (b, h, 0, 0))],
        out_specs=[pl.BlockSpec((tb, REC_D), lambda b, h, t: (b * nt + t, h)),
                   pl.BlockSpec((1, 1, REC_D, REC_D), lambda b, h, t: (b, h, 0, 0))],
        out_shape=[jax.ShapeDtypeStruct((out_rows, REC_W), BF16),
                   jax.ShapeDtypeStruct((n_seq, N_REC_HEADS, REC_D, REC_D), F32)],
        scratch_shapes=[pltpu.VMEM((REC_D, REC_D), F32)] + [pltpu.VMEM((CHUNK, REC_D), F32)] * 4,
        compiler_params=_cparams(("parallel", "parallel", "arbitrary")),
        name="hgrn2",
    )(proj, proj, proj, proj, lb, norm_w, s0)


def _merge_kernel(x_ref, oa_ref, or_ref, ga_ref, gr_ref, wa_ref, wr_ref, wo_ref, nf_ref,
                  rwh_ref, rwl_ref, rb_ref,
                  x1_ref, h2_ref, te_ref, gt_ref, rk_ref, cnt_ref, carry_scr):
    i = pl.program_id(0)

    @pl.when(i == 0)
    def _():
        carry_scr[...] = jnp.zeros_like(carry_scr)

    a = _dot(oa_ref[...], wa_ref[...])
    r = _dot(or_ref[...], wr_ref[...])
    merged = _sigmoid(ga_ref[...]) * a + _sigmoid(gr_ref[...]) * r
    x1 = x_ref[...] + _dot(merged.astype(BF16), wo_ref[...])
    x1_ref[...] = x1
    ms = jnp.mean(x1 * x1, axis=-1, keepdims=True)
    h2 = x1 * lax.rsqrt(ms + EPS) * nf_ref[...]
    h2_ref[...] = h2
    hh, hl = _split2(h2)
    logits = _dot(hh, rwh_ref[...]) + _dot(hl, rwh_ref[...]) + _dot(hh, rwl_ref[...]) + rb_ref[...]
    tm = logits.shape[0]
    lane = lax.broadcasted_iota(I32, (tm, LANES), 1)
    lane_f = lane.astype(F32)
    work = logits
    te = jnp.zeros((tm, LANES), I32)
    tv = jnp.zeros((tm, LANES), F32)
    onehot = jnp.zeros((tm, LANES), F32)
    picks = []
    for k in range(TOP_K):
        m = jnp.max(work, axis=-1, keepdims=True)
        idx = jnp.min(jnp.where(work == m, lane_f, float(LANES)), axis=-1, keepdims=True).astype(I32)
        sel = lane == idx
        work = jnp.where(sel, -jnp.inf, work)
        onehot = onehot + jnp.where(sel, 1.0, 0.0)
        te = jnp.where(lane == k, idx, te)
        tv = jnp.where(lane == k, m, tv)
        picks.append((m, sel))
    m0 = picks[0][0]
    ex = jnp.where(lane < TOP_K, jnp.exp(tv - m0), 0.0)
    gt_ref[...] = ex / jnp.sum(ex, axis=-1, keepdims=True)
    te_ref[...] = te
    rr = lax.broadcasted_iota(I32, (tm, tm), 0)
    cc = lax.broadcasted_iota(I32, (tm, tm), 1)
    strict = jnp.where(rr > cc, 1.0, 0.0).astype(BF16)
    before = _dot(strict, onehot.astype(BF16)) + carry_scr[...]
    rk = jnp.zeros((tm, LANES), F32)
    for k in range(TOP_K):
        rk_k = jnp.sum(jnp.where(picks[k][1], before, 0.0), axis=-1, keepdims=True)
        rk = jnp.where(lane == k, rk_k, rk)
    rk_ref[...] = rk.astype(I32)
    carry_scr[...] = carry_scr[...] + jnp.sum(onehot, axis=0, keepdims=True)
    cnt_ref[...] = carry_scr[...]


def _merge(x, oa, orr, proj, ga_blk, gr_blk, wa, wr, wo, nf, rwh, rwl, rb, tm):
    t, d = x.shape

    def const(shape):
        return pl.BlockSpec(shape, lambda i: (0,) * len(shape), pipeline_mode=pl.Buffered(1))

    return pl.pallas_call(
        _merge_kernel,
        grid=(t // tm,),
        in_specs=[pl.BlockSpec((tm, d), lambda i: (i, 0)),
                  pl.BlockSpec((tm, Q_W), lambda i: (i, 0)),
                  pl.BlockSpec((tm, REC_W), lambda i: (i, 0)),
                  pl.BlockSpec((tm, d), lambda i: (i, ga_blk)),
                  pl.BlockSpec((tm, d), lambda i: (i, gr_blk)),
                  const((Q_W, d)), const((REC_W, d)), const((d, d)), const((1, d)),
                  const((d, LANES)), const((d, LANES)), const((1, LANES))],
        out_specs=[pl.BlockSpec((tm, d), lambda i: (i, 0)),
                   pl.BlockSpec((tm, d), lambda i: (i, 0)),
                   pl.BlockSpec((tm, LANES), lambda i: (i, 0)),
                   pl.BlockSpec((tm, LANES), lambda i: (i, 0)),
                   pl.BlockSpec((tm, LANES), lambda i: (i, 0)),
                   pl.BlockSpec((1, LANES), lambda i: (0, 0))],
        out_shape=[jax.ShapeDtypeStruct((t, d), F32), jax.ShapeDtypeStruct((t, d), F32),
                   jax.ShapeDtypeStruct((t, LANES), I32), jax.ShapeDtypeStruct((t, LANES), F32),
                   jax.ShapeDtypeStruct((t, LANES), I32), jax.ShapeDtypeStruct((1, LANES), F32)],
        scratch_shapes=[pltpu.VMEM((1, LANES), F32)],
        compiler_params=_cparams(("arbitrary",)),
        name="merge_router",
    )(x, oa, orr, proj, proj, wa, wr, wo, nf, rwh, rwl, rb)


def _dispatch_kernel(slot_ref, h_ref, xs_in_ref, xs_ref, sem, *, n_tok):
    del xs_in_ref
    base = pl.program_id(0) * n_tok

    def copy(a):
        tok = base + a // TOP_K
        return pltpu.make_async_copy(h_ref.at[pl.ds(tok, 1)], xs_ref.at[pl.ds(slot_ref[0, a], 1)], sem)

    def start(a, c):
        copy(a).start()
        return c

    def wait(a, c):
        copy(a).wait()
        return c

    lax.fori_loop(0, n_tok * TOP_K, start, 0)
    lax.fori_loop(0, n_tok * TOP_K, wait, 0)


def _dispatch(slots, h2, n_rows, td):
    t, d = h2.shape
    xs0 = jnp.zeros((n_rows, d), h2.dtype)
    return pl.pallas_call(
        functools.partial(_dispatch_kernel, n_tok=td),
        grid=(t // td,),
        in_specs=[pl.BlockSpec((None, 1, td * TOP_K), lambda i: (i, 0, 0), memory_space=pltpu.SMEM),
                  pl.BlockSpec(memory_space=pl.ANY),
                  pl.BlockSpec(memory_space=pl.ANY)],
        out_specs=pl.BlockSpec(memory_space=pl.ANY),
        out_shape=jax.ShapeDtypeStruct((n_rows, d), h2.dtype),
        scratch_shapes=[pltpu.SemaphoreType.DMA],
        input_output_aliases={2: 0},
        compiler_params=_cparams(("arbitrary",)),
        name="moe_dispatch",
    )(slots.reshape(t // td, 1, td * TOP_K), h2, xs0)


def _moe_kernel(be_ref, na_ref, x_ref, wg_ref, wu_ref, wd_ref, bg_ref, bu_ref, bd_ref, o_ref, xb_scr):
    b = pl.program_id(0)
    f = pl.program_id(1)

    @pl.when(b < na_ref[0])
    def _():
        @pl.when(f == 0)
        def _():
            xb_scr[...] = x_ref[...].astype(BF16)
            o_ref[...] = jnp.broadcast_to(bd_ref[...], o_ref.shape)

        xb = xb_scr[...]
        g = _dot(xb, wg_ref[...].astype(BF16)) + bg_ref[...]
        u = _dot(xb, wu_ref[...].astype(BF16)) + bu_ref[...]
        g = jnp.minimum(g, SWIGLU_LIMIT)
        u = jnp.clip(u, -SWIGLU_LIMIT, SWIGLU_LIMIT)
        act = (u + 1.0) * (g * _sigmoid(SWIGLU_ALPHA * g))
        o_ref[...] += _dot(act.astype(BF16), wd_ref[...].astype(BF16))

    @pl.when(jnp.logical_and(b >= na_ref[0], f == 0))
    def _():
        o_ref[...] = jnp.zeros_like(o_ref)


def _moe_blocks(block_e, n_active, xs, w_gate_up, b_gate_up, w_down, b_down, bm, tf):
    n_rows, d = xs.shape
    n_exp, _, two_f = w_gate_up.shape
    d_ff = two_f // 2
    nb = n_rows // bm
    nf = d_ff // tf

    def blk(b, na):
        return jnp.minimum(b, na[0] - 1)

    def fi(b, f, na):
        return jnp.where(b < na[0], f, nf - 1)

    grid_spec = pltpu.PrefetchScalarGridSpec(
        num_scalar_prefetch=2,
        grid=(nb, nf),
        in_specs=[pl.BlockSpec((bm, d), lambda b, f, be, na: (blk(b, na), 0)),
                  pl.BlockSpec((None, d, tf), lambda b, f, be, na: (be[blk(b, na)], 0, fi(b, f, na))),
                  pl.BlockSpec((None, d, tf), lambda b, f, be, na: (be[blk(b, na)], 0, fi(b, f, na) + nf)),
                  pl.BlockSpec((None, tf, d), lambda b, f, be, na: (be[blk(b, na)], fi(b, f, na), 0)),
                  pl.BlockSpec((None, 1, tf), lambda b, f, be, na: (be[blk(b, na)], 0, fi(b, f, na))),
                  pl.BlockSpec((None, 1, tf), lambda b, f, be, na: (be[blk(b, na)], 0, fi(b, f, na) + nf)),
                  pl.BlockSpec((None, 1, d), lambda b, f, be, na: (be[blk(b, na)], 0, 0))],
        out_specs=pl.BlockSpec((bm, d), lambda b, f, be, na: (b, 0)),
        scratch_shapes=[pltpu.VMEM((bm, d), BF16)],
    )
    return pl.pallas_call(
        _moe_kernel,
        grid_spec=grid_spec,
        out_shape=jax.ShapeDtypeStruct((n_rows, d), F32),
        compiler_params=_cparams(("arbitrary", "arbitrary")),
        name="moe_experts",
    )(block_e, n_active, xs, w_gate_up, w_gate_up, w_down,
      b_gate_up.reshape(n_exp, 1, two_f), b_gate_up.reshape(n_exp, 1, two_f), b_down.reshape(n_exp, 1, d))


def _combine_kernel(slot_ref, x1_ref, gt_ref, os_ref, y_ref, buf, sem, *, n_tok):
    def copy(a):
        return pltpu.make_async_copy(os_ref.at[pl.ds(slot_ref[0, a], 1)],
                                     buf.at[a % TOP_K, pl.ds(a // TOP_K, 1)], sem)

    def start(a, c):
        copy(a).start()
        return c

    def wait(a, c):
        copy(a).wait()
        return c

    lax.fori_loop(0, n_tok * TOP_K, start, 0)
    lax.fori_loop(0, n_tok * TOP_K, wait, 0)
    y = x1_ref[...]
    gt = gt_ref[...]
    for k in range(TOP_K):
        y = y + gt[:, k:k + 1] * buf[k]
    y_ref[...] = y


def _combine(slots, x1, gates, o_sorted, tc):
    t, d = x1.shape
    return pl.pallas_call(
        functools.partial(_combine_kernel, n_tok=tc),
        grid=(t // tc,),
        in_specs=[pl.BlockSpec((None, 1, tc * TOP_K), lambda i: (i, 0, 0), memory_space=pltpu.SMEM),
                  pl.BlockSpec((tc, d), lambda i: (i, 0)),
                  pl.BlockSpec((tc, LANES), lambda i: (i, 0)),
                  pl.BlockSpec(memory_space=pl.ANY)],
        out_specs=pl.BlockSpec((tc, d), lambda i: (i, 0)),
        out_shape=jax.ShapeDtypeStruct((t, d), F32),
        scratch_shapes=[pltpu.VMEM((TOP_K, tc, d), F32), pltpu.SemaphoreType.DMA],
        compiler_params=_cparams(("arbitrary",)),
        name="moe_combine",
    )(slots.reshape(t // tc, 1, tc * TOP_K), x1, gates, o_sorted)


def _pick(n, pref):
    while n % pref:
        pref //= 2
    return pref


def _forward(x_prompt, x_sample, cache_k, cache_v, state_rec, norm_mix_w, w_in, q_norm_w, k_norm_w,
             attn_sinks, rec_lb_logits, rec_norm_w, w_attn_branch, w_rec_branch, w_out, norm_ffn_w,
             router_w, router_b, w_gate_up, b_gate_up, w_down, b_down, moe_bm=512, moe_tf=512):
    bp, sp, d = x_prompt.shape
    bs, ss, _ = x_sample.shape
    assert ss == CHUNK and sp % CHUNK == 0 and sp >= WINDOW
    tp, ts = bp * sp, bs * ss
    t = tp + ts
    x = jnp.concatenate([x_prompt.reshape(tp, d), x_sample.reshape(ts, d)], axis=0)

    w = w_in[0]
    o_q, o_k, o_v, o_r = 0, Q_W, Q_W + KV_W, Q_W + 2 * KV_W
    o_g = o_r + 4 * REC_W
    w_perm = jnp.concatenate([w[:, o_g:], w[:, o_q:o_k], w[:, o_r:o_g], w[:, o_k:o_r]], axis=1).astype(BF16)
    c_aq = 2 * d
    c_rec = c_aq + Q_W
    c_k = c_rec + 4 * REC_W
    c_v = c_k + KV_W

    proj = _in_proj(x, norm_mix_w, w_perm, _pick(t, 1024), 512)

    pos = jnp.concatenate([jnp.tile(jnp.arange(sp, dtype=I32), bp),
                           jnp.tile(PAST_LEN + jnp.arange(ss, dtype=I32), bs)]).astype(F32)
    half = HEAD_DIM // 2
    inv_freq = ROPE_THETA ** (-jnp.arange(half, dtype=F32) / half)
    ang = pos[:, None] * inv_freq[None, :]
    cos = jnp.tile(jnp.cos(ang), (1, LANES // half))
    sgn = jnp.tile(jnp.concatenate([-jnp.ones((half,), F32), jnp.ones((half,), F32)]), LANES // HEAD_DIM)
    sin = jnp.tile(jnp.sin(ang), (1, LANES // half)) * sgn[None, :]
    qw = jnp.tile(q_norm_w[0], LANES // HEAD_DIM)[None, :]
    kw = jnp.tile(k_norm_w[0], LANES // HEAD_DIM)[None, :]
    qr, kr = _qk_prep(proj, cos, sin, qw, kw, c_aq // Q_W, c_k // KV_W, _pick(t, 512))

    sinks = attn_sinks[0]
    npc = sp // CHUNK
    vcol = c_v // KV_W

    def prow(back):
        return lambda b, c: b * npc + jnp.maximum(c - back, 0)

    oa_p = _attention(sinks, qr, [(kr, 0)] * 3, [(proj, vcol)] * 3, (bp, npc),
                      lambda b, c: b * npc + c, lambda b, c: b * npc + c, tp,
                      [prow(2), prow(1), prow(0)], True)
    ck = cache_k[0].reshape(bs * WINDOW, KV_W)
    cv = cache_v[0].reshape(bs * WINDOW, KV_W)
    srow = tp // CHUNK
    oa_s = _attention(sinks, qr, [(ck, 0), (ck, 0), (kr, 0)], [(cv, 0), (cv, 0), (proj, vcol)], (bs,),
                      lambda b: srow + b, lambda b: b, ts,
                      [lambda b: 2 * b, lambda b: 2 * b + 1, lambda b: srow + b], False)
    oa = jnp.concatenate([oa_p, oa_s], axis=0)

    lb = jax.nn.softmax(rec_lb_logits.astype(F32), axis=0)[0][None, :]
    rec_cols = tuple((c_rec + i * REC_W) // REC_D for i in range(4))
    nw_rec = rec_norm_w[0][None, :]
    tb = _pick(sp, 512)
    or_p, sfin_p = _hgrn2(proj, lb, nw_rec, jnp.zeros((bp, N_REC_HEADS, REC_D, REC_D), F32),
                          0, bp, sp, tb, rec_cols, tp)
    or_s, sfin_s = _hgrn2(proj, lb, nw_rec, state_rec[0], tp // CHUNK, bs, ss, CHUNK, rec_cols, ts)
    orr = jnp.concatenate([or_p, or_s], axis=0)

    rw = jnp.pad(router_w[0], ((0, 0), (0, LANES - N_EXPERTS)))
    rwh = rw.astype(BF16)
    rwl = (rw - rwh.astype(F32)).astype(BF16)
    rb = jnp.concatenate([router_b[0].astype(F32), jnp.full((LANES - N_EXPERTS,), -jnp.inf, F32)])[None, :]
    x1, h2, te, gt, rk, cnt = _merge(x, oa, orr, proj, 0, 1, w_attn_branch[0].astype(BF16),
                                     w_rec_branch[0].astype(BF16), w_out[0].astype(BF16),
                                     norm_ffn_w, rwh, rwl, rb, _pick(t, 256))

    counts = cnt[0, :N_EXPERTS].astype(I32)
    padded = (counts + moe_bm - 1) // moe_bm * moe_bm
    pad_end = jnp.cumsum(padded)
    pad_start = pad_end - padded
    slots = pad_start[te[:, :TOP_K]] + rk[:, :TOP_K]
    nb = (t * TOP_K) // moe_bm + N_EXPERTS
    block_e = jnp.minimum(jnp.searchsorted(pad_end, jnp.arange(nb, dtype=I32) * moe_bm, side='right'),
                          N_EXPERTS - 1).astype(I32)
    n_active = (pad_end[-1:] // moe_bm).astype(I32)

    xs = _dispatch(slots, h2, nb * moe_bm, _pick(t, 512))
    o_sorted = _moe_blocks(block_e, n_active, xs, w_gate_up[0], b_gate_up[0], w_down[0], b_down[0],
                           moe_bm, moe_tf)
    y = _combine(slots, x1, gt, o_sorted, _pick(t, 128))

    y_p = y[:tp].reshape(bp, sp, d)
    y_s = y[tp:].reshape(bs, ss, d)
    v_new = proj[:, c_v:c_v + KV_W]
    kp = kr[:tp].reshape(bp, sp, N_KV_HEADS, HEAD_DIM)[:, -WINDOW:]
    vp = v_new[:tp].reshape(bp, sp, N_KV_HEADS, HEAD_DIM)[:, -WINDOW:]
    ks_new = kr[tp:].reshape(bs, ss, N_KV_HEADS, HEAD_DIM)
    vs_new = v_new[tp:].reshape(bs, ss, N_KV_HEADS, HEAD_DIM)
    ks = jnp.concatenate([cache_k[0], ks_new], axis=1)[:, -WINDOW:]
    vs = jnp.concatenate([cache_v[0], vs_new], axis=1)[:, -WINDOW:]
    return (y_p, y_s, kp[None], vp[None], sfin_p[None], ks[None], vs[None], sfin_s[None])


def kernel(x_prompt, x_sample, cache_k, cache_v, state_rec, norm_mix_w, w_in, q_norm_w, k_norm_w, attn_sinks, rec_lb_logits, rec_norm_w, w_attn_branch, w_rec_branch, w_out, norm_ffn_w, router_w, router_b, w_gate_up, b_gate_up, w_down, b_down):
    return _forward(x_prompt, x_sample, cache_k, cache_v, state_rec, norm_mix_w, w_in, q_norm_w, k_norm_w,
                    attn_sinks, rec_lb_logits, rec_norm_w, w_attn_branch, w_rec_branch, w_out, norm_ffn_w,
                    router_w, router_b, w_gate_up, b_gate_up, w_down, b_down)
```

```python
import functools

import jax
import jax.numpy as jnp
from jax import lax
from jax.experimental import pallas as pl
from jax.experimental.pallas import tpu as pltpu

F32 = jnp.float32
BF16 = jnp.bfloat16
I32 = jnp.int32

CHUNK = 64
N_HEADS = 16
N_KV_HEADS = 4
HEAD_DIM = 64
WINDOW = 128
ROPE_THETA = 10000.0
N_REC_HEADS = 8
REC_D = 128
N_EXPERTS = 32
TOP_K = 4
SWIGLU_LIMIT = 7.0
SWIGLU_ALPHA = 1.702
EPS = 1e-6
PAST_LEN = 2048
LANES = 128
SUB = 16
PACK_ROWS = 8
OUT_ROWS = 16

VMEM_LIMIT = 56 * 1024 * 1024

Q_W = N_HEADS * HEAD_DIM
KV_W = N_KV_HEADS * HEAD_DIM
REC_W = N_REC_HEADS * REC_D


def _cparams(sem):
    return pltpu.CompilerParams(dimension_semantics=sem, vmem_limit_bytes=VMEM_LIMIT)


def _sigmoid(x):
    return 1.0 / (1.0 + jnp.exp(-x))


def _split2(x):
    hi = x.astype(BF16)
    lo = (x - hi.astype(F32)).astype(BF16)
    return hi, lo


def _dot(a, b):
    return jnp.dot(a, b, preferred_element_type=F32)


def _dot_nt(a, b):
    return lax.dot_general(a, b, (((1,), (1,)), ((), ())), preferred_element_type=F32)


def _in_proj_kernel(x_ref, nw_ref, w_ref, o_ref, h_scr):
    @pl.when(pl.program_id(1) == 0)
    def _():
        x = x_ref[...]
        ms = jnp.mean(x * x, axis=-1, keepdims=True)
        h_scr[...] = (x * lax.rsqrt(ms + EPS) * nw_ref[...]).astype(BF16)

    o_ref[...] = _dot(h_scr[...], w_ref[...])


def _in_proj(x, norm_w, w, tm, tn):
    t, d = x.shape
    n = w.shape[1]
    return pl.pallas_call(
        _in_proj_kernel,
        grid=(t // tm, n // tn),
        in_specs=[pl.BlockSpec((tm, d), lambda i, j: (i, 0)),
                  pl.BlockSpec((1, d), lambda i, j: (0, 0)),
                  pl.BlockSpec((d, tn), lambda i, j: (0, j))],
        out_specs=pl.BlockSpec((tm, tn), lambda i, j: (i, j)),
        out_shape=jax.ShapeDtypeStruct((t, n), F32),
        scratch_shapes=[pltpu.VMEM((tm, d), BF16)],
        compiler_params=_cparams(("parallel", "arbitrary")),
        name="in_proj",
    )(x, norm_w, w)


def _norm_rope(x, nw, cos, sin, bd):
    hi, lo = _split2(x * x)
    ss = _dot(hi, bd) + _dot(lo, bd)
    y = x * lax.rsqrt(ss * (1.0 / HEAD_DIM) + EPS) * nw
    lane = lax.broadcasted_iota(I32, y.shape, 1)
    first_half = (lane % HEAD_DIM) < (HEAD_DIM // 2)
    rot = jnp.where(first_half, pltpu.roll(y, LANES - HEAD_DIM // 2, 1), pltpu.roll(y, HEAD_DIM // 2, 1))
    return y * cos + rot * sin


def _qk_prep_kernel(q_ref, k_ref, cos_ref, sin_ref, qw_ref, kw_ref, qo_ref, ko_ref):
    r = lax.broadcasted_iota(I32, (LANES, LANES), 0) // HEAD_DIM
    c = lax.broadcasted_iota(I32, (LANES, LANES), 1) // HEAD_DIM
    bd = jnp.where(r == c, 1.0, 0.0).astype(BF16)
    cos = cos_ref[...]
    sin = sin_ref[...]
    scale = HEAD_DIM ** -0.5
    for g in range(Q_W // LANES):
        sl = slice(g * LANES, (g + 1) * LANES)
        qo_ref[:, sl] = (_norm_rope(q_ref[:, sl], qw_ref[...], cos, sin, bd) * scale).astype(BF16)
    for g in range(KV_W // LANES):
        sl = slice(g * LANES, (g + 1) * LANES)
        ko_ref[:, sl] = _norm_rope(k_ref[:, sl], kw_ref[...], cos, sin, bd)


def _qk_prep(proj, cos, sin, qw, kw, q_blk, k_blk, tr):
    t = proj.shape[0]
    return pl.pallas_call(
        _qk_prep_kernel,
        grid=(t // tr,),
        in_specs=[pl.BlockSpec((tr, Q_W), lambda i: (i, q_blk)),
                  pl.BlockSpec((tr, KV_W), lambda i: (i, k_blk)),
                  pl.BlockSpec((tr, LANES), lambda i: (i, 0)),
                  pl.BlockSpec((tr, LANES), lambda i: (i, 0)),
                  pl.BlockSpec((1, LANES), lambda i: (0, 0)),
                  pl.BlockSpec((1, LANES), lambda i: (0, 0))],
        out_specs=[pl.BlockSpec((tr, Q_W), lambda i: (i, 0)),
                   pl.BlockSpec((tr, KV_W), lambda i: (i, 0))],
        out_shape=[jax.ShapeDtypeStruct((t, Q_W), BF16), jax.ShapeDtypeStruct((t, KV_W), F32)],
        compiler_params=_cparams(("parallel",)),
        name="qk_prep",
    )(proj, proj, cos, sin, qw, kw)


def _attn_kernel(sink_ref, q_ref, k0_ref, k1_ref, k2_ref, v0_ref, v1_ref, v2_ref, o_ref, *, masked):
    k_all = jnp.concatenate([k0_ref[...], k1_ref[...], k2_ref[...]], axis=0)
    v_all = jnp.concatenate([v0_ref[...], v1_ref[...], v2_ref[...]], axis=0)
    n_keys = 3 * CHUNK
    lane = lax.broadcasted_iota(I32, (n_keys, LANES), 1)
    low = lane < HEAD_DIM
    row = lax.broadcasted_iota(I32, (2 * CHUNK, 1), 0)
    if masked:
        c = pl.program_id(1)
        key_chunk = lax.broadcasted_iota(I32, (2 * CHUNK, n_keys), 1) // CHUNK
        valid = (key_chunk + c) >= 2
    for j in range(N_KV_HEADS):
        pair = slice((j // 2) * LANES, (j // 2 + 1) * LANES)
        kp = k_all[:, pair]
        vp = v_all[:, pair]
        kr = pltpu.roll(kp, HEAD_DIM, 1)
        vr = pltpu.roll(vp, HEAD_DIM, 1)
        if j % 2 == 0:
            k_low, k_high, v_low, v_high = kp, kr, vp, vr
        else:
            k_low, k_high, v_low, v_high = kr, kp, vr, vp
        q2 = jnp.concatenate([q_ref[:, (2 * j) * LANES:(2 * j + 1) * LANES],
                              q_ref[:, (2 * j + 1) * LANES:(2 * j + 2) * LANES]], axis=0)
        o = jnp.zeros((2 * CHUNK, LANES), F32)
        for half, (kk, vv) in enumerate(((k_low, v_low), (k_high, v_high))):
            keep = low if half == 0 else jnp.logical_not(low)
            kz = jnp.where(keep, kk, 0.0).astype(BF16)
            vz = jnp.where(keep, vv, 0.0).astype(BF16)
            s = _dot_nt(q2, kz)
            if masked:
                s = jnp.where(valid, s, -jnp.inf)
            sink = jnp.where(row < CHUNK, sink_ref[4 * j + half], sink_ref[4 * j + 2 + half])
            m = jnp.maximum(jnp.max(s, axis=-1, keepdims=True), sink)
            p = jnp.exp(s - m)
            p = p / (jnp.sum(p, axis=-1, keepdims=True) + jnp.exp(sink - m))
            o = o + _dot(p.astype(BF16), vz)
        o_ref[:, (2 * j) * LANES:(2 * j + 1) * LANES] = o[:CHUNK].astype(BF16)
        o_ref[:, (2 * j + 1) * LANES:(2 * j + 2) * LANES] = o[CHUNK:].astype(BF16)


def _attention(sinks, qr, k_srcs, v_srcs, grid, q_map, o_map, n_rows, kv_maps, masked):
    def spec(col, fn):
        return pl.BlockSpec((CHUNK, KV_W), lambda *g: (fn(*g), col))

    in_specs = [pl.BlockSpec(memory_space=pltpu.SMEM),
                pl.BlockSpec((CHUNK, Q_W), lambda *g: (q_map(*g), 0))]
    in_specs += [spec(col, fn) for (_, col), fn in zip(k_srcs, kv_maps)]
    in_specs += [spec(col, fn) for (_, col), fn in zip(v_srcs, kv_maps)]
    return pl.pallas_call(
        functools.partial(_attn_kernel, masked=masked),
        grid=grid,
        in_specs=in_specs,
        out_specs=pl.BlockSpec((CHUNK, Q_W), lambda *g: (o_map(*g), 0)),
        out_shape=jax.ShapeDtypeStruct((n_rows, Q_W), BF16),
        compiler_params=_cparams(("parallel",) * len(grid)),
        name="attn_prompt" if masked else "attn_sample",
    )(sinks, qr, *[a for a, _ in k_srcs], *[a for a, _ in v_srcs])


HEADS_PER_STEP = 4


def _hgrn2_chunk(r0, sl, hh, rq_ref, rf_ref, rv_ref, rg_ref, lb_ref, nw_ref, o_ref,
                 st_scr, q_scr, k_scr, v_scr, c_scr, tri, ones, rsub):
    n_sub = CHUNK // SUB
    rq = rq_ref[pl.ds(r0, CHUNK), sl]
    q = rq * _sigmoid(rq)
    k = (1.0 - lb_ref[:, sl]) * _sigmoid(-rf_ref[pl.ds(r0, CHUNK), sl])
    g = jnp.log1p(-k)
    v = rv_ref[pl.ds(r0, CHUNK), sl]
    g1 = g.astype(BF16)
    e1 = g - g1.astype(F32)
    g2 = e1.astype(BF16)
    g3 = (e1 - g2.astype(F32)).astype(BF16)
    cum = _dot(tri, g1) + _dot(tri, g2) + _dot(tri, g3)
    q_scr[hh] = q
    k_scr[hh] = k
    v_scr[hh] = v
    c_scr[hh] = cum
    st = st_scr[hh]
    inter = _dot_nt((q * jnp.exp(cum)).astype(BF16), st.astype(BF16))
    outs = []
    for i in range(n_sub):
        lo = i * SUB
        q_i = q_scr[hh, lo:lo + SUB, :]
        c_i = c_scr[hh, lo:lo + SUB, :]
        acc = inter[lo:lo + SUB, :]
        if i > 0:
            ref = c_scr[hh, lo - 1:lo, :]
            qd = (q_i * jnp.exp(c_i - ref)).astype(BF16)
            kd = (k_scr[hh, 0:lo, :] * jnp.exp(ref - c_scr[hh, 0:lo, :])).astype(BF16)
            s = _dot_nt(qd, kd)
            acc = acc + _dot(s.astype(BF16), v_scr[hh, 0:lo, :].astype(BF16))
        parts = []
        for s_ in range(SUB):
            arg = jnp.where(rsub >= s_, c_i - c_scr[hh, lo + s_:lo + s_ + 1, :], -jnp.inf)
            parts.append(q_i * k_scr[hh, lo + s_:lo + s_ + 1, :] * jnp.exp(arg))
        dsum = _dot(jnp.concatenate(parts, axis=0).astype(BF16), ones)
        for s_ in range(SUB):
            acc = acc + dsum[s_ * SUB:(s_ + 1) * SUB, :] * v_scr[hh, lo + s_:lo + s_ + 1, :]
        outs.append(acc)
    o = jnp.concatenate(outs, axis=0)
    last = c_scr[hh, CHUNK - 1:CHUNK, :]
    kdec = (k * jnp.exp(last - cum)).astype(BF16)
    st_scr[hh] = st * jnp.exp(last) + _dot(v.T.astype(BF16), kdec)
    ms = jnp.mean(o * o, axis=-1, keepdims=True)
    rg = rg_ref[pl.ds(r0, CHUNK), sl]
    o_ref[pl.ds(r0, CHUNK), sl] = (o * lax.rsqrt(ms + EPS) * nw_ref[...] * (rg * _sigmoid(rg))).astype(BF16)


def _hgrn2_kernel(rq_ref, rf_ref, rv_ref, rg_ref, lb_ref, nw_ref, s0_ref, o_ref, sfin_ref,
                  st_scr, q_scr, k_scr, v_scr, c_scr, *, n_chunks):
    t = pl.program_id(2)

    @pl.when(t == 0)
    def _():
        for hh in range(HEADS_PER_STEP):
            st_scr[hh] = s0_ref[0, hh].T

    r64 = lax.broadcasted_iota(I32, (CHUNK, CHUNK), 0)
    c64 = lax.broadcasted_iota(I32, (CHUNK, CHUNK), 1)
    tri = jnp.where(r64 >= c64, 1.0, 0.0).astype(BF16)
    ones = jnp.ones((LANES, LANES), BF16)
    rsub = lax.broadcasted_iota(I32, (SUB, LANES), 0)

    def chunk(ci, carry):
        r0 = pl.multiple_of(ci * CHUNK, CHUNK)
        for hh in range(HEADS_PER_STEP):
            sl = slice(hh * REC_D, (hh + 1) * REC_D)
            _hgrn2_chunk(r0, sl, hh, rq_ref, rf_ref, rv_ref, rg_ref, lb_ref, nw_ref, o_ref,
                         st_scr, q_scr, k_scr, v_scr, c_scr, tri, ones, rsub)
        return carry

    lax.fori_loop(0, n_chunks, chunk, 0)

    @pl.when(t == pl.num_programs(2) - 1)
    def _():
        for hh in range(HEADS_PER_STEP):
            sfin_ref[0, hh] = st_scr[hh].T


def _hgrn2(proj, lb, norm_w, s0, row_blk0, n_seq, seq_len, tb, col_blks, out_rows):
    nt = seq_len // tb
    hb = HEADS_PER_STEP
    wb = hb * REC_D
    cq, cf, cv, cg = col_blks

    def in_spec(c0):
        return pl.BlockSpec((tb, wb), lambda b, h, t: (row_blk0 + b * nt + t, c0 // wb + h))

    return pl.pallas_call(
        functools.partial(_hgrn2_kernel, n_chunks=tb // CHUNK),
        grid=(n_seq, N_REC_HEADS // hb, nt),
        in_specs=[in_spec(cq), in_spec(cf), in_spec(cv), in_spec(cg),
                  pl.BlockSpec((1, wb), lambda b, h, t: (0, h)),
                  pl.BlockSpec((1, REC_D), lambda b, h, t: (0, 0)),
                  pl.BlockSpec((1, hb, REC_D, REC_D), lambda b, h, t: (b, h, 0, 0))],
        out_specs=[pl.BlockSpec((tb, wb), lambda b, h, t: (b * nt + t, h)),
                   pl.BlockSpec((1, hb, REC_D, REC_D), lambda b, h, t: (b, h, 0, 0))],
        out_shape=[jax.ShapeDtypeStruct((out_rows, REC_W), BF16),
                   jax.ShapeDtypeStruct((n_seq, N_REC_HEADS, REC_D, REC_D), F32)],
        scratch_shapes=[pltpu.VMEM((hb, REC_D, REC_D), F32)] + [pltpu.VMEM((hb, CHUNK, REC_D), F32)] * 4,
        compiler_params=_cparams(("parallel", "parallel", "arbitrary")),
        name="hgrn2",
    )(proj, proj, proj, proj, lb, norm_w, s0)


def _merge_kernel(x_ref, oa_ref, or_ref, ga_ref, gr_ref, wa_ref, wr_ref, wo_ref, nf_ref,
                  rwh_ref, rwl_ref, rb_ref,
                  x1_ref, h2_ref, te_ref, gt_ref, rk_ref, cnt_ref, carry_scr):
    i = pl.program_id(0)

    @pl.when(i == 0)
    def _():
        carry_scr[...] = jnp.zeros_like(carry_scr)

    a = _dot(oa_ref[...], wa_ref[...])
    r = _dot(or_ref[...], wr_ref[...])
    merged = _sigmoid(ga_ref[...]) * a + _sigmoid(gr_ref[...]) * r
    x1 = x_ref[...] + _dot(merged.astype(BF16), wo_ref[...])
    x1_ref[...] = x1
    ms = jnp.mean(x1 * x1, axis=-1, keepdims=True)
    h2 = x1 * lax.rsqrt(ms + EPS) * nf_ref[...]
    tm, d = h2.shape
    bits = lax.bitcast_convert_type(h2.astype(BF16).astype(F32), jnp.uint32)
    packed = (bits[:, d // 2:] & jnp.uint32(0xFFFF0000)) | (bits[:, :d // 2] >> 16)
    for g in range(PACK_ROWS):
        h2_ref[pl.ds(g, tm, stride=PACK_ROWS), :] = packed[:, g * LANES:(g + 1) * LANES]
    hh, hl = _split2(h2)
    logits = _dot(hh, rwh_ref[...]) + _dot(hl, rwh_ref[...]) + _dot(hh, rwl_ref[...]) + rb_ref[...]
    lane = lax.broadcasted_iota(I32, (tm, LANES), 1)
    lane_f = lane.astype(F32)
    work = logits
    te = jnp.zeros((tm, LANES), I32)
    tv = jnp.zeros((tm, LANES), F32)
    onehot = jnp.zeros((tm, LANES), F32)
    picks = []
    for k in range(TOP_K):
        m = jnp.max(work, axis=-1, keepdims=True)
        idx = jnp.min(jnp.where(work == m, lane_f, float(LANES)), axis=-1, keepdims=True).astype(I32)
        sel = lane == idx
        work = jnp.where(sel, -jnp.inf, work)
        onehot = onehot + jnp.where(sel, 1.0, 0.0)
        te = jnp.where(lane == k, idx, te)
        tv = jnp.where(lane == k, m, tv)
        picks.append((m, sel))
    m0 = picks[0][0]
    ex = jnp.where(lane < TOP_K, jnp.exp(tv - m0), 0.0)
    gt_ref[...] = ex / jnp.sum(ex, axis=-1, keepdims=True)
    te_ref[...] = te
    rr = lax.broadcasted_iota(I32, (tm, tm), 0)
    cc = lax.broadcasted_iota(I32, (tm, tm), 1)
    strict = jnp.where(rr > cc, 1.0, 0.0).astype(BF16)
    before = _dot(strict, onehot.astype(BF16)) + carry_scr[...]
    rk = jnp.zeros((tm, LANES), F32)
    for k in range(TOP_K):
        rk_k = jnp.sum(jnp.where(picks[k][1], before, 0.0), axis=-1, keepdims=True)
        rk = jnp.where(lane == k, rk_k, rk)
    rk_ref[...] = rk.astype(I32)
    carry_scr[...] = carry_scr[...] + jnp.sum(onehot, axis=0, keepdims=True)
    cnt_ref[...] = carry_scr[...]


def _merge(x, oa, orr, proj, ga_blk, gr_blk, wa, wr, wo, nf, rwh, rwl, rb, tm):
    t, d = x.shape

    def const(shape):
        return pl.BlockSpec(shape, lambda i: (0,) * len(shape), pipeline_mode=pl.Buffered(1))

    return pl.pallas_call(
        _merge_kernel,
        grid=(t // tm,),
        in_specs=[pl.BlockSpec((tm, d), lambda i: (i, 0)),
                  pl.BlockSpec((tm, Q_W), lambda i: (i, 0)),
                  pl.BlockSpec((tm, REC_W), lambda i: (i, 0)),
                  pl.BlockSpec((tm, d), lambda i: (i, ga_blk)),
                  pl.BlockSpec((tm, d), lambda i: (i, gr_blk)),
                  const((Q_W, d)), const((REC_W, d)), const((d, d)), const((1, d)),
                  const((d, LANES)), const((d, LANES)), const((1, LANES))],
        out_specs=[pl.BlockSpec((tm, d), lambda i: (i, 0)),
                   pl.BlockSpec((tm * PACK_ROWS, LANES), lambda i: (i, 0)),
                   pl.BlockSpec((tm, LANES), lambda i: (i, 0)),
                   pl.BlockSpec((tm, LANES), lambda i: (i, 0)),
                   pl.BlockSpec((tm, LANES), lambda i: (i, 0)),
                   pl.BlockSpec((1, LANES), lambda i: (0, 0))],
        out_shape=[jax.ShapeDtypeStruct((t, d), F32), jax.ShapeDtypeStruct((t * PACK_ROWS, LANES), jnp.uint32),
                   jax.ShapeDtypeStruct((t, LANES), I32), jax.ShapeDtypeStruct((t, LANES), F32),
                   jax.ShapeDtypeStruct((t, LANES), I32), jax.ShapeDtypeStruct((1, LANES), F32)],
        scratch_shapes=[pltpu.VMEM((1, LANES), F32)],
        compiler_params=_cparams(("arbitrary",)),
        name="merge_router",
    )(x, oa, orr, proj, proj, wa, wr, wo, nf, rwh, rwl, rb)


def _dispatch_kernel(slot_ref, h_ref, xs_in_ref, xs_ref, sem, *, n_tok):
    del xs_in_ref

    def copy(a):
        src = pl.multiple_of((a // TOP_K) * PACK_ROWS, PACK_ROWS)
        dst = pl.multiple_of(slot_ref[0, a] * PACK_ROWS, PACK_ROWS)
        return pltpu.make_async_copy(h_ref.at[pl.ds(src, PACK_ROWS)], xs_ref.at[pl.ds(dst, PACK_ROWS)], sem)

    def start(a, c):
        copy(a).start()
        return c

    def wait(a, c):
        copy(a).wait()
        return c

    lax.fori_loop(0, n_tok * TOP_K, start, 0)
    lax.fori_loop(0, n_tok * TOP_K, wait, 0)


def _dispatch(slots, h2p, n_rows, td):
    t = h2p.shape[0] // PACK_ROWS
    xs0 = jnp.zeros((n_rows * PACK_ROWS, LANES), h2p.dtype)
    return pl.pallas_call(
        functools.partial(_dispatch_kernel, n_tok=td),
        grid=(t // td,),
        in_specs=[pl.BlockSpec((None, 1, td * TOP_K), lambda i: (i, 0, 0), memory_space=pltpu.SMEM),
                  pl.BlockSpec((td * PACK_ROWS, LANES), lambda i: (i, 0)),
                  pl.BlockSpec(memory_space=pl.ANY)],
        out_specs=pl.BlockSpec(memory_space=pl.ANY),
        out_shape=jax.ShapeDtypeStruct(xs0.shape, xs0.dtype),
        scratch_shapes=[pltpu.SemaphoreType.DMA],
        input_output_aliases={2: 0},
        compiler_params=_cparams(("arbitrary",)),
        name="moe_dispatch",
    )(slots.reshape(t // td, 1, td * TOP_K), h2p, xs0)


def _moe_kernel(be_ref, na_ref, x_ref, wg_ref, wu_ref, wd_ref, bg_ref, bu_ref, bd_ref, o_ref, xb_scr, acc_scr):
    b = pl.program_id(0)
    f = pl.program_id(1)
    bm, d = acc_scr.shape

    @pl.when(b < na_ref[0])
    def _():
        @pl.when(f == 0)
        def _():
            for g in range(PACK_ROWS):
                pk = x_ref[pl.ds(g, bm, stride=PACK_ROWS), :]
                lo = lax.bitcast_convert_type(pk << 16, F32)
                hi = lax.bitcast_convert_type(pk & jnp.uint32(0xFFFF0000), F32)
                xb_scr[:, g * LANES:(g + 1) * LANES] = lo.astype(BF16)
                xb_scr[:, d // 2 + g * LANES:d // 2 + (g + 1) * LANES] = hi.astype(BF16)
            acc_scr[...] = jnp.broadcast_to(bd_ref[...], acc_scr.shape)

        xb = xb_scr[...]
        g = _dot(xb, wg_ref[...].astype(BF16)) + bg_ref[...]
        u = _dot(xb, wu_ref[...].astype(BF16)) + bu_ref[...]
        g = jnp.minimum(g, SWIGLU_LIMIT)
        u = jnp.clip(u, -SWIGLU_LIMIT, SWIGLU_LIMIT)
        act = (u + 1.0) * (g * _sigmoid(SWIGLU_ALPHA * g))
        acc_scr[...] += _dot(act.astype(BF16), wd_ref[...].astype(BF16))

        @pl.when(f == pl.num_programs(1) - 1)
        def _():
            for g in range(OUT_ROWS):
                o_ref[pl.ds(g, bm, stride=OUT_ROWS), :] = acc_scr[:, g * LANES:(g + 1) * LANES]

    @pl.when(jnp.logical_and(b >= na_ref[0], f == 0))
    def _():
        o_ref[...] = jnp.zeros_like(o_ref)


def _moe_blocks(block_e, n_active, xs, w_gate_up, b_gate_up, w_down, b_down, bm, tf):
    n_exp, d, two_f = w_gate_up.shape
    assert d == 2 * PACK_ROWS * LANES and d == OUT_ROWS * LANES
    n_rows = xs.shape[0] // PACK_ROWS
    d_ff = two_f // 2
    nb = n_rows // bm
    nf = d_ff // tf

    def blk(b, na):
        return jnp.minimum(b, na[0] - 1)

    def fi(b, f, na):
        return jnp.where(b < na[0], f, nf - 1)

    grid_spec = pltpu.PrefetchScalarGridSpec(
        num_scalar_prefetch=2,
        grid=(nb, nf),
        in_specs=[pl.BlockSpec((bm * PACK_ROWS, LANES), lambda b, f, be, na: (blk(b, na), 0)),
                  pl.BlockSpec((None, d, tf), lambda b, f, be, na: (be[blk(b, na)], 0, fi(b, f, na))),
                  pl.BlockSpec((None, d, tf), lambda b, f, be, na: (be[blk(b, na)], 0, fi(b, f, na) + nf)),
                  pl.BlockSpec((None, tf, d), lambda b, f, be, na: (be[blk(b, na)], fi(b, f, na), 0)),
                  pl.BlockSpec((None, 1, tf), lambda b, f, be, na: (be[blk(b, na)], 0, fi(b, f, na))),
                  pl.BlockSpec((None, 1, tf), lambda b, f, be, na: (be[blk(b, na)], 0, fi(b, f, na) + nf)),
                  pl.BlockSpec((None, 1, d), lambda b, f, be, na: (be[blk(b, na)], 0, 0))],
        out_specs=pl.BlockSpec((bm * OUT_ROWS, LANES), lambda b, f, be, na: (b, 0)),
        scratch_shapes=[pltpu.VMEM((bm, d), BF16), pltpu.VMEM((bm, d), F32)],
    )
    return pl.pallas_call(
        _moe_kernel,
        grid_spec=grid_spec,
        out_shape=jax.ShapeDtypeStruct((n_rows * OUT_ROWS, LANES), F32),
        compiler_params=_cparams(("arbitrary", "arbitrary")),
        name="moe_experts",
    )(block_e, n_active, xs, w_gate_up, w_gate_up, w_down,
      b_gate_up.reshape(n_exp, 1, two_f), b_gate_up.reshape(n_exp, 1, two_f), b_down.reshape(n_exp, 1, d))


def _combine_kernel(slot_ref, x1_ref, gt_ref, os_ref, y_ref, buf, sem, *, n_tok):
    def copy(a):
        src = pl.multiple_of(slot_ref[0, a] * OUT_ROWS, OUT_ROWS)
        dst = pl.multiple_of((a // TOP_K) * OUT_ROWS, OUT_ROWS)
        return pltpu.make_async_copy(os_ref.at[pl.ds(src, OUT_ROWS)], buf.at[a % TOP_K, pl.ds(dst, OUT_ROWS)], sem)

    def start(a, c):
        copy(a).start()
        return c

    def wait(a, c):
        copy(a).wait()
        return c

    lax.fori_loop(0, n_tok * TOP_K, start, 0)
    lax.fori_loop(0, n_tok * TOP_K, wait, 0)
    gt = gt_ref[...]
    for g in range(OUT_ROWS):
        sl = slice(g * LANES, (g + 1) * LANES)
        y = x1_ref[:, sl]
        for k in range(TOP_K):
            y = y + gt[:, k:k + 1] * buf[k, pl.ds(g, n_tok, stride=OUT_ROWS), :]
        y_ref[:, sl] = y


def _combine(slots, x1, gates, o_sorted, tc):
    t, d = x1.shape
    return pl.pallas_call(
        functools.partial(_combine_kernel, n_tok=tc),
        grid=(t // tc,),
        in_specs=[pl.BlockSpec((None, 1, tc * TOP_K), lambda i: (i, 0, 0), memory_space=pltpu.SMEM),
                  pl.BlockSpec((tc, d), lambda i: (i, 0)),
                  pl.BlockSpec((tc, LANES), lambda i: (i, 0)),
                  pl.BlockSpec(memory_space=pl.ANY)],
        out_specs=pl.BlockSpec((tc, d), lambda i: (i, 0)),
        out_shape=jax.ShapeDtypeStruct((t, d), F32),
        scratch_shapes=[pltpu.VMEM((TOP_K, tc * OUT_ROWS, LANES), F32), pltpu.SemaphoreType.DMA],
        compiler_params=_cparams(("arbitrary",)),
        name="moe_combine",
    )(slots.reshape(t // tc, 1, tc * TOP_K), x1, gates, o_sorted)


def _pick(n, pref):
    while n % pref:
        pref //= 2
    return pref


def _forward(x_prompt, x_sample, cache_k, cache_v, state_rec, norm_mix_w, w_in, q_norm_w, k_norm_w,
             attn_sinks, rec_lb_logits, rec_norm_w, w_attn_branch, w_rec_branch, w_out, norm_ffn_w,
             router_w, router_b, w_gate_up, b_gate_up, w_down, b_down, moe_bm=512, moe_tf=512):
    bp, sp, d = x_prompt.shape
    bs, ss, _ = x_sample.shape
    assert ss == CHUNK and sp % CHUNK == 0 and sp >= WINDOW
    tp, ts = bp * sp, bs * ss
    t = tp + ts
    x = jnp.concatenate([x_prompt.reshape(tp, d), x_sample.reshape(ts, d)], axis=0)

    w = w_in[0]
    o_q, o_k, o_v, o_r = 0, Q_W, Q_W + KV_W, Q_W + 2 * KV_W
    o_g = o_r + 4 * REC_W
    w_perm = jnp.concatenate([w[:, o_g:], w[:, o_q:o_k], w[:, o_r:o_g], w[:, o_k:o_r]], axis=1).astype(BF16)
    c_aq = 2 * d
    c_rec = c_aq + Q_W
    c_k = c_rec + 4 * REC_W
    c_v = c_k + KV_W

    proj = _in_proj(x, norm_mix_w, w_perm, _pick(t, 1024), 512)

    pos = jnp.concatenate([jnp.tile(jnp.arange(sp, dtype=I32), bp),
                           jnp.tile(PAST_LEN + jnp.arange(ss, dtype=I32), bs)]).astype(F32)
    half = HEAD_DIM // 2
    inv_freq = ROPE_THETA ** (-jnp.arange(half, dtype=F32) / half)
    ang = pos[:, None] * inv_freq[None, :]
    cos = jnp.tile(jnp.cos(ang), (1, LANES // half))
    sgn = jnp.tile(jnp.concatenate([-jnp.ones((half,), F32), jnp.ones((half,), F32)]), LANES // HEAD_DIM)
    sin = jnp.tile(jnp.sin(ang), (1, LANES // half)) * sgn[None, :]
    qw = jnp.tile(q_norm_w[0], LANES // HEAD_DIM)[None, :]
    kw = jnp.tile(k_norm_w[0], LANES // HEAD_DIM)[None, :]
    qr, kr = _qk_prep(proj, cos, sin, qw, kw, c_aq // Q_W, c_k // KV_W, _pick(t, 512))

    sinks = attn_sinks[0]
    npc = sp // CHUNK
    vcol = c_v // KV_W

    def prow(back):
        return lambda b, c: b * npc + jnp.maximum(c - back, 0)

    oa_p = _attention(sinks, qr, [(kr, 0)] * 3, [(proj, vcol)] * 3, (bp, npc),
                      lambda b, c: b * npc + c, lambda b, c: b * npc + c, tp,
                      [prow(2), prow(1), prow(0)], True)
    ck = cache_k[0].reshape(bs * WINDOW, KV_W)
    cv = cache_v[0].reshape(bs * WINDOW, KV_W)
    srow = tp // CHUNK
    oa_s = _attention(sinks, qr, [(ck, 0), (ck, 0), (kr, 0)], [(cv, 0), (cv, 0), (proj, vcol)], (bs,),
                      lambda b: srow + b, lambda b: b, ts,
                      [lambda b: 2 * b, lambda b: 2 * b + 1, lambda b: srow + b], False)
    oa = jnp.concatenate([oa_p, oa_s], axis=0)

    lb = jax.nn.softmax(rec_lb_logits.astype(F32), axis=0)[0][None, :]
    rec_cols = tuple(c_rec + i * REC_W for i in range(4))
    nw_rec = rec_norm_w[0][None, :]
    tb = _pick(sp, 512)
    or_p, sfin_p = _hgrn2(proj, lb, nw_rec, jnp.zeros((bp, N_REC_HEADS, REC_D, REC_D), F32),
                          0, bp, sp, tb, rec_cols, tp)
    or_s, sfin_s = _hgrn2(proj, lb, nw_rec, state_rec[0], tp // CHUNK, bs, ss, CHUNK, rec_cols, ts)
    orr = jnp.concatenate([or_p, or_s], axis=0)

    rw = jnp.pad(router_w[0], ((0, 0), (0, LANES - N_EXPERTS)))
    rwh = rw.astype(BF16)
    rwl = (rw - rwh.astype(F32)).astype(BF16)
    rb = jnp.concatenate([router_b[0].astype(F32), jnp.full((LANES - N_EXPERTS,), -jnp.inf, F32)])[None, :]
    x1, h2p, te, gt, rk, cnt = _merge(x, oa, orr, proj, 0, 1, w_attn_branch[0].astype(BF16),
                                      w_rec_branch[0].astype(BF16), w_out[0].astype(BF16),
                                      norm_ffn_w, rwh, rwl, rb, _pick(t, 256))

    counts = cnt[0, :N_EXPERTS].astype(I32)
    padded = (counts + moe_bm - 1) // moe_bm * moe_bm
    pad_end = jnp.cumsum(padded)
    pad_start = pad_end - padded
    slots = pad_start[te[:, :TOP_K]] + rk[:, :TOP_K]
    nb = (t * TOP_K) // moe_bm + N_EXPERTS
    block_start = jnp.arange(nb, dtype=I32) * moe_bm
    block_e = jnp.minimum(jnp.sum((pad_end[None, :] <= block_start[:, None]).astype(I32), axis=1), N_EXPERTS - 1)
    n_active = (pad_end[-1:] // moe_bm).astype(I32)

    xs = _dispatch(slots, h2p, nb * moe_bm, _pick(t, 512))
    o_sorted = _moe_blocks(block_e, n_active, xs, w_gate_up[0], b_gate_up[0], w_down[0], b_down[0],
                           moe_bm, moe_tf)
    y = _combine(slots, x1, gt, o_sorted, _pick(t, 128))

    y_p = y[:tp].reshape(bp, sp, d)
    y_s = y[tp:].reshape(bs, ss, d)
    v_new = proj[:, c_v:c_v + KV_W]
    kp = kr[:tp].reshape(bp, sp, N_KV_HEADS, HEAD_DIM)[:, -WINDOW:]
    vp = v_new[:tp].reshape(bp, sp, N_KV_HEADS, HEAD_DIM)[:, -WINDOW:]
    ks_new = kr[tp:].reshape(bs, ss, N_KV_HEADS, HEAD_DIM)
    vs_new = v_new[tp:].reshape(bs, ss, N_KV_HEADS, HEAD_DIM)
    ks = jnp.concatenate([cache_k[0], ks_new], axis=1)[:, -WINDOW:]
    vs = jnp.concatenate([cache_v[0], vs_new], axis=1)[:, -WINDOW:]
    return (y_p, y_s, kp[None], vp[None], sfin_p[None], ks[None], vs[None], sfin_s[None])


def kernel(x_prompt, x_sample, cache_k, cache_v, state_rec, norm_mix_w, w_in, q_norm_w, k_norm_w, attn_sinks, rec_lb_logits, rec_norm_w, w_attn_branch, w_rec_branch, w_out, norm_ffn_w, router_w, router_b, w_gate_up, b_gate_up, w_down, b_down):
    return _forward(x_prompt, x_sample, cache_k, cache_v, state_rec, norm_mix_w, w_in, q_norm_w, k_norm_w,
                    attn_sinks, rec_lb_logits, rec_norm_w, w_attn_branch, w_rec_branch, w_out, norm_ffn_w,
                    router_w, router_b, w_gate_up, b_gate_up, w_down, b_down)
```

```python
import functools

import jax
import jax.numpy as jnp
from jax import lax
from jax.experimental import pallas as pl
from jax.experimental.pallas import tpu as pltpu

F32 = jnp.float32
BF16 = jnp.bfloat16
I32 = jnp.int32

CHUNK = 64
N_HEADS = 16
N_KV_HEADS = 4
HEAD_DIM = 64
WINDOW = 128
ROPE_THETA = 10000.0
N_REC_HEADS = 8
REC_D = 128
N_EXPERTS = 32
TOP_K = 4
SWIGLU_LIMIT = 7.0
SWIGLU_ALPHA = 1.702
EPS = 1e-6
PAST_LEN = 2048
LANES = 128
SUB = 16

VMEM_LIMIT = 56 * 1024 * 1024

Q_W = N_HEADS * HEAD_DIM
KV_W = N_KV_HEADS * HEAD_DIM
REC_W = N_REC_HEADS * REC_D


def _cparams(sem):
    return pltpu.CompilerParams(dimension_semantics=sem, vmem_limit_bytes=VMEM_LIMIT)


def _sigmoid(x):
    return 1.0 / (1.0 + jnp.exp(-x))


def _split2(x):
    hi = x.astype(BF16)
    lo = (x - hi.astype(F32)).astype(BF16)
    return hi, lo


def _dot(a, b):
    return jnp.dot(a, b, preferred_element_type=F32)


def _dot_nt(a, b):
    return lax.dot_general(a, b, (((1,), (1,)), ((), ())), preferred_element_type=F32)


def _two_group_specs(shape, n_first):
    first = pl.BlockSpec(shape, lambda i, *_: (jnp.minimum(i, n_first - 1), 0))
    second = pl.BlockSpec(shape, lambda i, *_: (jnp.maximum(i - n_first, 0), 0))
    return first, second


def _in_proj_kernel(xa_ref, xb_ref, nw_ref, w_ref, o_ref, h_scr, *, n_first):
    def norm(x_ref):
        x = x_ref[...]
        ms = jnp.mean(x * x, axis=-1, keepdims=True)
        h_scr[...] = (x * lax.rsqrt(ms + EPS) * nw_ref[...]).astype(BF16)

    first_col = pl.program_id(1) == 0
    in_first = pl.program_id(0) < n_first
    pl.when(jnp.logical_and(first_col, in_first))(lambda: norm(xa_ref))
    pl.when(jnp.logical_and(first_col, jnp.logical_not(in_first)))(lambda: norm(xb_ref))
    o_ref[...] = _dot(h_scr[...], w_ref[...])


def _in_proj(xa, xb, norm_w, w, tm, tn):
    d = xa.shape[1]
    t = xa.shape[0] + xb.shape[0]
    n = w.shape[1]
    n_first = xa.shape[0] // tm
    return pl.pallas_call(
        functools.partial(_in_proj_kernel, n_first=n_first),
        grid=(t // tm, n // tn),
        in_specs=[*_two_group_specs((tm, d), n_first),
                  pl.BlockSpec((1, d), lambda i, j: (0, 0)),
                  pl.BlockSpec((d, tn), lambda i, j: (0, j))],
        out_specs=pl.BlockSpec((tm, tn), lambda i, j: (i, j)),
        out_shape=jax.ShapeDtypeStruct((t, n), F32),
        scratch_shapes=[pltpu.VMEM((tm, d), BF16)],
        compiler_params=_cparams(("parallel", "arbitrary")),
        name="in_proj",
    )(xa, xb, norm_w, w)


def _norm_rope(x, nw, cos, sin, bd):
    hi, lo = _split2(x * x)
    ss = _dot(hi, bd) + _dot(lo, bd)
    y = x * lax.rsqrt(ss * (1.0 / HEAD_DIM) + EPS) * nw
    lane = lax.broadcasted_iota(I32, y.shape, 1)
    first_half = (lane % HEAD_DIM) < (HEAD_DIM // 2)
    rot = jnp.where(first_half, pltpu.roll(y, LANES - HEAD_DIM // 2, 1), pltpu.roll(y, HEAD_DIM // 2, 1))
    return y * cos + rot * sin


def _qk_prep_kernel(q_ref, k_ref, cos_ref, sin_ref, qw_ref, kw_ref, qo_ref, ko_ref):
    r = lax.broadcasted_iota(I32, (LANES, LANES), 0) // HEAD_DIM
    c = lax.broadcasted_iota(I32, (LANES, LANES), 1) // HEAD_DIM
    bd = jnp.where(r == c, 1.0, 0.0).astype(BF16)
    cos = cos_ref[...]
    sin = sin_ref[...]
    scale = HEAD_DIM ** -0.5
    for g in range(Q_W // LANES):
        sl = slice(g * LANES, (g + 1) * LANES)
        qo_ref[:, sl] = (_norm_rope(q_ref[:, sl], qw_ref[...], cos, sin, bd) * scale).astype(BF16)
    for g in range(KV_W // LANES):
        sl = slice(g * LANES, (g + 1) * LANES)
        ko_ref[:, sl] = _norm_rope(k_ref[:, sl], kw_ref[...], cos, sin, bd)


def _qk_prep(proj, cos, sin, qw, kw, q_blk, k_blk, tr):
    t = proj.shape[0]
    return pl.pallas_call(
        _qk_prep_kernel,
        grid=(t // tr,),
        in_specs=[pl.BlockSpec((tr, Q_W), lambda i: (i, q_blk)),
                  pl.BlockSpec((tr, KV_W), lambda i: (i, k_blk)),
                  pl.BlockSpec((tr, LANES), lambda i: (i, 0)),
                  pl.BlockSpec((tr, LANES), lambda i: (i, 0)),
                  pl.BlockSpec((1, LANES), lambda i: (0, 0)),
                  pl.BlockSpec((1, LANES), lambda i: (0, 0))],
        out_specs=[pl.BlockSpec((tr, Q_W), lambda i: (i, 0)),
                   pl.BlockSpec((tr, KV_W), lambda i: (i, 0))],
        out_shape=[jax.ShapeDtypeStruct((t, Q_W), BF16), jax.ShapeDtypeStruct((t, KV_W), F32)],
        compiler_params=_cparams(("parallel",)),
        name="qk_prep",
    )(proj, proj, cos, sin, qw, kw)


def _attn_kernel(sink_ref, q_ref, k0_ref, k1_ref, k2_ref, v0_ref, v1_ref, v2_ref, o_ref, *, masked):
    k_all = jnp.concatenate([k0_ref[...], k1_ref[...], k2_ref[...]], axis=0)
    v_all = jnp.concatenate([v0_ref[...], v1_ref[...], v2_ref[...]], axis=0)
    n_keys = 3 * CHUNK
    lane = lax.broadcasted_iota(I32, (n_keys, LANES), 1)
    low = lane < HEAD_DIM
    row = lax.broadcasted_iota(I32, (2 * CHUNK, 1), 0)
    if masked:
        c = pl.program_id(1)
        key_chunk = lax.broadcasted_iota(I32, (2 * CHUNK, n_keys), 1) // CHUNK
        valid = (key_chunk + c) >= 2
    for j in range(N_KV_HEADS):
        pair = slice((j // 2) * LANES, (j // 2 + 1) * LANES)
        kp = k_all[:, pair]
        vp = v_all[:, pair]
        kr = pltpu.roll(kp, HEAD_DIM, 1)
        vr = pltpu.roll(vp, HEAD_DIM, 1)
        if j % 2 == 0:
            k_low, k_high, v_low, v_high = kp, kr, vp, vr
        else:
            k_low, k_high, v_low, v_high = kr, kp, vr, vp
        q2 = jnp.concatenate([q_ref[:, (2 * j) * LANES:(2 * j + 1) * LANES],
                              q_ref[:, (2 * j + 1) * LANES:(2 * j + 2) * LANES]], axis=0)
        o = jnp.zeros((2 * CHUNK, LANES), F32)
        for half, (kk, vv) in enumerate(((k_low, v_low), (k_high, v_high))):
            keep = low if half == 0 else jnp.logical_not(low)
            kz = jnp.where(keep, kk, 0.0).astype(BF16)
            vz = jnp.where(keep, vv, 0.0).astype(BF16)
            s = _dot_nt(q2, kz)
            if masked:
                s = jnp.where(valid, s, -jnp.inf)
            sink = jnp.where(row < CHUNK, sink_ref[4 * j + half], sink_ref[4 * j + 2 + half])
            m = jnp.maximum(jnp.max(s, axis=-1, keepdims=True), sink)
            p = jnp.exp(s - m)
            p = p / (jnp.sum(p, axis=-1, keepdims=True) + jnp.exp(sink - m))
            o = o + _dot(p.astype(BF16), vz)
        o_ref[:, (2 * j) * LANES:(2 * j + 1) * LANES] = o[:CHUNK].astype(BF16)
        o_ref[:, (2 * j + 1) * LANES:(2 * j + 2) * LANES] = o[CHUNK:].astype(BF16)


def _attention(sinks, qr, k_srcs, v_srcs, grid, q_map, o_map, n_rows, kv_maps, masked):
    def spec(col, fn):
        return pl.BlockSpec((CHUNK, KV_W), lambda *g: (fn(*g), col))

    in_specs = [pl.BlockSpec(memory_space=pltpu.SMEM),
                pl.BlockSpec((CHUNK, Q_W), lambda *g: (q_map(*g), 0))]
    in_specs += [spec(col, fn) for (_, col), fn in zip(k_srcs, kv_maps)]
    in_specs += [spec(col, fn) for (_, col), fn in zip(v_srcs, kv_maps)]
    return pl.pallas_call(
        functools.partial(_attn_kernel, masked=masked),
        grid=grid,
        in_specs=in_specs,
        out_specs=pl.BlockSpec((CHUNK, Q_W), lambda *g: (o_map(*g), 0)),
        out_shape=jax.ShapeDtypeStruct((n_rows, Q_W), BF16),
        compiler_params=_cparams(("parallel",) * len(grid)),
        name="attn_prompt" if masked else "attn_sample",
    )(sinks, qr, *[a for a, _ in k_srcs], *[a for a, _ in v_srcs])


HEADS_PER_STEP = 4


def _hgrn2_chunk(r0, sl, hh, rq_ref, rf_ref, rv_ref, rg_ref, lb_ref, nw_ref, o_ref,
                 st_scr, q_scr, k_scr, v_scr, c_scr, tri, ones, rsub):
    n_sub = CHUNK // SUB
    rq = rq_ref[pl.ds(r0, CHUNK), sl]
    q = rq * _sigmoid(rq)
    k = (1.0 - lb_ref[:, sl]) * _sigmoid(-rf_ref[pl.ds(r0, CHUNK), sl])
    g = jnp.log1p(-k)
    v = rv_ref[pl.ds(r0, CHUNK), sl]
    g1 = g.astype(BF16)
    e1 = g - g1.astype(F32)
    g2 = e1.astype(BF16)
    g3 = (e1 - g2.astype(F32)).astype(BF16)
    cum = _dot(tri, g1) + _dot(tri, g2) + _dot(tri, g3)
    q_scr[hh] = q
    k_scr[hh] = k
    v_scr[hh] = v
    c_scr[hh] = cum
    st = st_scr[hh]
    inter = _dot_nt((q * jnp.exp(cum)).astype(BF16), st.astype(BF16))
    outs = []
    for i in range(n_sub):
        lo = i * SUB
        q_i = q_scr[hh, lo:lo + SUB, :]
        c_i = c_scr[hh, lo:lo + SUB, :]
        acc = inter[lo:lo + SUB, :]
        if i > 0:
            ref = c_scr[hh, lo - 1:lo, :]
            qd = (q_i * jnp.exp(c_i - ref)).astype(BF16)
            kd = (k_scr[hh, 0:lo, :] * jnp.exp(ref - c_scr[hh, 0:lo, :])).astype(BF16)
            s = _dot_nt(qd, kd)
            acc = acc + _dot(s.astype(BF16), v_scr[hh, 0:lo, :].astype(BF16))
        parts = []
        for s_ in range(SUB):
            arg = jnp.where(rsub >= s_, c_i - c_scr[hh, lo + s_:lo + s_ + 1, :], -jnp.inf)
            parts.append(q_i * k_scr[hh, lo + s_:lo + s_ + 1, :] * jnp.exp(arg))
        dsum = _dot(jnp.concatenate(parts, axis=0).astype(BF16), ones)
        for s_ in range(SUB):
            acc = acc + dsum[s_ * SUB:(s_ + 1) * SUB, :] * v_scr[hh, lo + s_:lo + s_ + 1, :]
        outs.append(acc)
    o = jnp.concatenate(outs, axis=0)
    last = c_scr[hh, CHUNK - 1:CHUNK, :]
    kdec = (k * jnp.exp(last - cum)).astype(BF16)
    st_scr[hh] = st * jnp.exp(last) + _dot(v.T.astype(BF16), kdec)
    ms = jnp.mean(o * o, axis=-1, keepdims=True)
    rg = rg_ref[pl.ds(r0, CHUNK), sl]
    o_ref[pl.ds(r0, CHUNK), sl] = (o * lax.rsqrt(ms + EPS) * nw_ref[...] * (rg * _sigmoid(rg))).astype(BF16)


def _hgrn2_kernel(rq_ref, rf_ref, rv_ref, rg_ref, lb_ref, nw_ref, s0_ref, o_ref, sfin_ref,
                  st_scr, q_scr, k_scr, v_scr, c_scr, *, n_chunks):
    t = pl.program_id(2)

    @pl.when(t == 0)
    def _():
        for hh in range(HEADS_PER_STEP):
            st_scr[hh] = s0_ref[0, hh].T

    r64 = lax.broadcasted_iota(I32, (CHUNK, CHUNK), 0)
    c64 = lax.broadcasted_iota(I32, (CHUNK, CHUNK), 1)
    tri = jnp.where(r64 >= c64, 1.0, 0.0).astype(BF16)
    ones = jnp.ones((LANES, LANES), BF16)
    rsub = lax.broadcasted_iota(I32, (SUB, LANES), 0)

    def chunk(ci, carry):
        r0 = pl.multiple_of(ci * CHUNK, CHUNK)
        for hh in range(HEADS_PER_STEP):
            sl = slice(hh * REC_D, (hh + 1) * REC_D)
            _hgrn2_chunk(r0, sl, hh, rq_ref, rf_ref, rv_ref, rg_ref, lb_ref, nw_ref, o_ref,
                         st_scr, q_scr, k_scr, v_scr, c_scr, tri, ones, rsub)
        return carry

    lax.fori_loop(0, n_chunks, chunk, 0)

    @pl.when(t == pl.num_programs(2) - 1)
    def _():
        for hh in range(HEADS_PER_STEP):
            sfin_ref[0, hh] = st_scr[hh].T


def _hgrn2(proj, lb, norm_w, s0, row_blk0, n_seq, seq_len, tb, col_blks, out_rows):
    nt = seq_len // tb
    hb = HEADS_PER_STEP
    wb = hb * REC_D
    cq, cf, cv, cg = col_blks

    def in_spec(c0):
        return pl.BlockSpec((tb, wb), lambda b, h, t: (row_blk0 + b * nt + t, c0 // wb + h))

    return pl.pallas_call(
        functools.partial(_hgrn2_kernel, n_chunks=tb // CHUNK),
        grid=(n_seq, N_REC_HEADS // hb, nt),
        in_specs=[in_spec(cq), in_spec(cf), in_spec(cv), in_spec(cg),
                  pl.BlockSpec((1, wb), lambda b, h, t: (0, h)),
                  pl.BlockSpec((1, REC_D), lambda b, h, t: (0, 0)),
                  pl.BlockSpec((1, hb, REC_D, REC_D), lambda b, h, t: (b, h, 0, 0))],
        out_specs=[pl.BlockSpec((tb, wb), lambda b, h, t: (b * nt + t, h)),
                   pl.BlockSpec((1, hb, REC_D, REC_D), lambda b, h, t: (b, h, 0, 0))],
        out_shape=[jax.ShapeDtypeStruct((out_rows, REC_W), BF16),
                   jax.ShapeDtypeStruct((n_seq, N_REC_HEADS, REC_D, REC_D), F32)],
        scratch_shapes=[pltpu.VMEM((hb, REC_D, REC_D), F32)] + [pltpu.VMEM((hb, CHUNK, REC_D), F32)] * 4,
        compiler_params=_cparams(("parallel", "parallel", "arbitrary")),
        name="hgrn2",
    )(proj, proj, proj, proj, lb, norm_w, s0)


def _pack_bf16_pairs(h):
    d = h.shape[1]
    bits = lax.bitcast_convert_type(h, jnp.uint32)
    return (bits[:, d // 2:] & jnp.uint32(0xFFFF0000)) | (bits[:, :d // 2] >> 16)


def _merge_kernel(xa_ref, xb_ref, oaa_ref, oab_ref, ora_ref, orb_ref, ga_ref, gr_ref, wa_ref, wr_ref, wo_ref,
                  nf_ref, rwh_ref, rwl_ref, rb_ref,
                  x1_ref, h2_ref, gt_ref, te_ref, lr_ref, tcnt_ref, tpre_ref, carry_scr, *, n_first):
    i = pl.program_id(0)

    @pl.when(i == 0)
    def _():
        carry_scr[...] = jnp.zeros_like(carry_scr)

    in_first = i < n_first
    a = _dot(jnp.where(in_first, oaa_ref[...], oab_ref[...]), wa_ref[...])
    r = _dot(jnp.where(in_first, ora_ref[...], orb_ref[...]), wr_ref[...])
    merged = _sigmoid(ga_ref[...]) * a + _sigmoid(gr_ref[...]) * r
    x1 = jnp.where(in_first, xa_ref[...], xb_ref[...]) + _dot(merged.astype(BF16), wo_ref[...])
    x1_ref[...] = x1
    ms = jnp.mean(x1 * x1, axis=-1, keepdims=True)
    h2 = x1 * lax.rsqrt(ms + EPS) * nf_ref[...]
    tm = h2.shape[0]
    hh, hl = _split2(h2)
    logits = _dot(hh, rwh_ref[...]) + _dot(hl, rwh_ref[...]) + _dot(hh, rwl_ref[...]) + rb_ref[...]
    lane = lax.broadcasted_iota(I32, (tm, LANES), 1)
    lane_f = lane.astype(F32)
    work = logits
    h2_ref[...] = hh
    te = jnp.zeros((tm, LANES), I32)
    tv = jnp.zeros((tm, LANES), F32)
    onehot = jnp.zeros((tm, LANES), F32)
    sels = []
    for k in range(TOP_K):
        m = jnp.max(work, axis=-1, keepdims=True)
        idx = jnp.min(jnp.where(work == m, lane_f, float(LANES)), axis=-1, keepdims=True).astype(I32)
        sel = lane == idx
        work = jnp.where(sel, -jnp.inf, work)
        onehot = onehot + jnp.where(sel, 1.0, 0.0)
        te = jnp.where(lane == k, idx, te)
        tv = jnp.where(lane == k, m, tv)
        sels.append(sel)
    ex = jnp.where(lane < TOP_K, jnp.exp(tv - tv[:, 0:1]), 0.0)
    gt_ref[...] = ex / jnp.sum(ex, axis=-1, keepdims=True)
    te_ref[...] = te
    rr = lax.broadcasted_iota(I32, (tm, tm), 0)
    cc = lax.broadcasted_iota(I32, (tm, tm), 1)
    strict = jnp.where(rr > cc, 1.0, 0.0).astype(BF16)
    lrank = _dot(strict, onehot.astype(BF16))
    lr = jnp.zeros((tm, LANES), F32)
    for k in range(TOP_K):
        lr = jnp.where(lane == k, jnp.sum(jnp.where(sels[k], lrank, 0.0), axis=-1, keepdims=True), lr)
    lr_ref[...] = lr.astype(I32)
    tcnt = jnp.sum(onehot, axis=0, keepdims=True)
    tcnt_ref[0] = tcnt.astype(I32)
    tpre_ref[0] = carry_scr[...].astype(I32)
    carry_scr[...] = carry_scr[...] + tcnt


def _merge(xa, xb, oaa, oab, ora, orb, proj, ga_blk, gr_blk, wa, wr, wo, nf, rwh, rwl, rb, tm):
    d = xa.shape[1]
    t = xa.shape[0] + xb.shape[0]
    nt = t // tm
    n_first = xa.shape[0] // tm

    def const(shape):
        return pl.BlockSpec(shape, lambda i: (0,) * len(shape), pipeline_mode=pl.Buffered(1))

    tile_vec = pl.BlockSpec((1, 1, LANES), lambda i: (i, 0, 0))
    return pl.pallas_call(
        functools.partial(_merge_kernel, n_first=n_first),
        grid=(nt,),
        in_specs=[*_two_group_specs((tm, d), n_first),
                  *_two_group_specs((tm, Q_W), n_first),
                  *_two_group_specs((tm, REC_W), n_first),
                  pl.BlockSpec((tm, d), lambda i: (i, ga_blk)),
                  pl.BlockSpec((tm, d), lambda i: (i, gr_blk)),
                  const((Q_W, d)), const((REC_W, d)), const((d, d)), const((1, d)),
                  const((d, LANES)), const((d, LANES)), const((1, LANES))],
        out_specs=[pl.BlockSpec((tm, d), lambda i: (i, 0)),
                   pl.BlockSpec((tm, d), lambda i: (i, 0)),
                   pl.BlockSpec((tm, LANES), lambda i: (i, 0)),
                   pl.BlockSpec((tm, LANES), lambda i: (i, 0)),
                   pl.BlockSpec((tm, LANES), lambda i: (i, 0)),
                   tile_vec, tile_vec],
        out_shape=[jax.ShapeDtypeStruct((t, d), F32), jax.ShapeDtypeStruct((t, d), BF16),
                   jax.ShapeDtypeStruct((t, LANES), F32), jax.ShapeDtypeStruct((t, LANES), I32),
                   jax.ShapeDtypeStruct((t, LANES), I32),
                   jax.ShapeDtypeStruct((nt, 1, LANES), I32), jax.ShapeDtypeStruct((nt, 1, LANES), I32)],
        scratch_shapes=[pltpu.VMEM((1, LANES), F32)],
        compiler_params=_cparams(("arbitrary",)),
        name="merge_router",
    )(xa, xb, oaa, oab, ora, orb, proj, proj, wa, wr, wo, nf, rwh, rwl, rb)


SUBLANES = 8
RUN_BITS = 9


def _local_rows(tm):
    return TOP_K * tm + SUBLANES * N_EXPERTS


def _run_layout(cnt_ref, glb_ref, tile, loc_scr):
    def body(e, nxt):
        j = tile * N_EXPERTS + e
        loc = nxt + ((glb_ref[j] - nxt) & (SUBLANES - 1))
        loc_scr[e] = loc
        return loc + cnt_ref[j]

    lax.fori_loop(0, N_EXPERTS, body, 0)


def _run_copies(cnt_ref, glb_ref, loc_scr, tile, make_copy, action):
    def per_expert(e, c):
        j = tile * N_EXPERTS + e
        n = cnt_ref[j]
        glb = glb_ref[j]
        loc = loc_scr[e]
        head = jnp.minimum(n, (-glb) & (SUBLANES - 1))
        mid = ((n - head) // SUBLANES) * SUBLANES

        def single(i, c2):
            action(make_copy(loc + i, glb + i, 1))
            return c2

        lax.fori_loop(0, head, single, 0)
        for bit in range(RUN_BITS - 1, 2, -1):
            size = 1 << bit

            @pl.when((mid & size) != 0)
            def _():
                done = head + ((mid >> (bit + 1)) << (bit + 1))
                action(make_copy(pl.multiple_of(loc + done, SUBLANES), pl.multiple_of(glb + done, SUBLANES), size))

        lax.fori_loop(head + mid, n, single, 0)
        return c

    lax.fori_loop(0, N_EXPERTS, per_expert, 0)


def _positions(te, lr, loc_scr, n_local):
    tm = te.shape[0]
    lane = lax.broadcasted_iota(I32, (1, LANES), 1)
    lvec = jnp.zeros((1, LANES), I32)
    for e in range(N_EXPERTS):
        lvec = jnp.where(lane == e, loc_scr[e], lvec)
    lvec = lvec.astype(F32)
    lane_t = lax.broadcasted_iota(I32, (tm, LANES), 1)
    pos = []
    for k in range(TOP_K):
        start = jnp.sum(jnp.where(lane_t == te[:, k:k + 1], lvec, 0.0), axis=-1, keepdims=True)
        pos.append(start.astype(I32) + lr[:, k:k + 1])
    return pos, lax.broadcasted_iota(I32, (tm, n_local), 1)


def _dispatch_kernel(cnt_ref, glb_ref, h_ref, te_ref, lr_ref, xs_in_ref, xs_ref, srt_scr, loc_scr, sem):
    del xs_in_ref
    tile = pl.program_id(0)
    _run_layout(cnt_ref, glb_ref, tile, loc_scr)
    pos, slot_iota = _positions(te_ref[...], lr_ref[...], loc_scr, srt_scr.shape[0])
    perm_t = jnp.zeros(slot_iota.shape, F32)
    for k in range(TOP_K):
        perm_t = perm_t + jnp.where(slot_iota == pos[k], 1.0, 0.0)
    rows = lax.dot_general(perm_t.astype(BF16), h_ref[...], (((0,), (0,)), ((), ())), preferred_element_type=F32)
    srt_scr[...] = _pack_bf16_pairs(rows)

    def make_copy(loc, glb, size):
        return pltpu.make_async_copy(srt_scr.at[pl.ds(loc, size)], xs_ref.at[pl.ds(glb, size)], sem)

    _run_copies(cnt_ref, glb_ref, loc_scr, tile, make_copy, lambda cp: cp.start())
    _run_copies(cnt_ref, glb_ref, loc_scr, tile, make_copy, lambda cp: cp.wait())


def _dispatch(cnt, glb, h2, te, lr, n_rows, tm):
    t, d = h2.shape
    assert tm < (1 << RUN_BITS)
    xs0 = jnp.zeros((n_rows, d // 2), jnp.uint32)
    grid_spec = pltpu.PrefetchScalarGridSpec(
        num_scalar_prefetch=2,
        grid=(t // tm,),
        in_specs=[pl.BlockSpec((tm, d), lambda i, *_: (i, 0)),
                  pl.BlockSpec((tm, LANES), lambda i, *_: (i, 0)),
                  pl.BlockSpec((tm, LANES), lambda i, *_: (i, 0)),
                  pl.BlockSpec(memory_space=pl.ANY)],
        out_specs=pl.BlockSpec(memory_space=pl.ANY),
        scratch_shapes=[pltpu.VMEM((_local_rows(tm), d // 2), jnp.uint32), pltpu.SMEM((N_EXPERTS,), I32),
                        pltpu.SemaphoreType.DMA],
    )
    return pl.pallas_call(
        _dispatch_kernel,
        grid_spec=grid_spec,
        out_shape=jax.ShapeDtypeStruct(xs0.shape, xs0.dtype),
        input_output_aliases={5: 0},
        compiler_params=_cparams(("arbitrary",)),
        name="moe_dispatch",
    )(cnt, glb, h2, te, lr, xs0)


def _moe_kernel(be_ref, na_ref, x_ref, wg_ref, wu_ref, wd_ref, bg_ref, bu_ref, bd_ref, o_ref, xb_scr):
    b = pl.program_id(0)
    f = pl.program_id(1)
    half = x_ref.shape[1]

    @pl.when(b < na_ref[0])
    def _():
        @pl.when(f == 0)
        def _():
            pk = x_ref[...]
            xb_scr[:, :half] = lax.bitcast_convert_type(pk << 16, F32).astype(BF16)
            xb_scr[:, half:] = lax.bitcast_convert_type(pk & jnp.uint32(0xFFFF0000), F32).astype(BF16)
            o_ref[...] = jnp.broadcast_to(bd_ref[...], o_ref.shape)

        xb = xb_scr[...]
        g = _dot(xb, wg_ref[...].astype(BF16)) + bg_ref[...]
        u = _dot(xb, wu_ref[...].astype(BF16)) + bu_ref[...]
        g = jnp.minimum(g, SWIGLU_LIMIT)
        u = jnp.clip(u, -SWIGLU_LIMIT, SWIGLU_LIMIT)
        act = (u + 1.0) * (g * _sigmoid(SWIGLU_ALPHA * g))
        o_ref[...] += _dot(act.astype(BF16), wd_ref[...].astype(BF16))

    @pl.when(jnp.logical_and(b >= na_ref[0], f == 0))
    def _():
        o_ref[...] = jnp.zeros_like(o_ref)


def _moe_blocks(block_e, n_active, xs, w_gate_up, b_gate_up, w_down, b_down, bm, tf):
    n_exp, d, two_f = w_gate_up.shape
    n_rows = xs.shape[0]
    d_ff = two_f // 2
    nb = n_rows // bm
    nf = d_ff // tf

    def blk(b, na):
        return jnp.minimum(b, na[0] - 1)

    def fi(b, f, na):
        return jnp.where(b < na[0], f, nf - 1)

    grid_spec = pltpu.PrefetchScalarGridSpec(
        num_scalar_prefetch=2,
        grid=(nb, nf),
        in_specs=[pl.BlockSpec((bm, d // 2), lambda b, f, be, na: (blk(b, na), 0)),
                  pl.BlockSpec((None, d, tf), lambda b, f, be, na: (be[blk(b, na)], 0, fi(b, f, na))),
                  pl.BlockSpec((None, d, tf), lambda b, f, be, na: (be[blk(b, na)], 0, fi(b, f, na) + nf)),
                  pl.BlockSpec((None, tf, d), lambda b, f, be, na: (be[blk(b, na)], fi(b, f, na), 0)),
                  pl.BlockSpec((None, 1, tf), lambda b, f, be, na: (be[blk(b, na)], 0, fi(b, f, na))),
                  pl.BlockSpec((None, 1, tf), lambda b, f, be, na: (be[blk(b, na)], 0, fi(b, f, na) + nf)),
                  pl.BlockSpec((None, 1, d), lambda b, f, be, na: (be[blk(b, na)], 0, 0))],
        out_specs=pl.BlockSpec((bm, d), lambda b, f, be, na: (b, 0)),
        scratch_shapes=[pltpu.VMEM((bm, d), BF16)],
    )
    return pl.pallas_call(
        _moe_kernel,
        grid_spec=grid_spec,
        out_shape=jax.ShapeDtypeStruct((n_rows, d), F32),
        compiler_params=_cparams(("arbitrary", "arbitrary")),
        name="moe_experts",
    )(block_e, n_active, xs, w_gate_up, w_gate_up, w_down,
      b_gate_up.reshape(n_exp, 1, two_f), b_gate_up.reshape(n_exp, 1, two_f), b_down.reshape(n_exp, 1, d))


def _combine_kernel(cnt_ref, glb_ref, x1_ref, gt_ref, te_ref, lr_ref, os_ref, ya_ref, yb_ref, buf, loc_scr, sem, *,
                    n_first):
    tile = pl.program_id(0)

    @pl.when(tile == 0)
    def _():
        buf[...] = jnp.zeros_like(buf)

    _run_layout(cnt_ref, glb_ref, tile, loc_scr)

    def make_copy(loc, glb, size):
        return pltpu.make_async_copy(os_ref.at[pl.ds(glb, size)], buf.at[pl.ds(loc, size)], sem)

    _run_copies(cnt_ref, glb_ref, loc_scr, tile, make_copy, lambda cp: cp.start())
    pos, slot_iota = _positions(te_ref[...], lr_ref[...], loc_scr, buf.shape[0])
    gt = gt_ref[...]
    wsel = jnp.zeros(slot_iota.shape, F32)
    for k in range(TOP_K):
        wsel = wsel + jnp.where(slot_iota == pos[k], gt[:, k:k + 1], 0.0)
    _run_copies(cnt_ref, glb_ref, loc_scr, tile, make_copy, lambda cp: cp.wait())
    wh, wl = _split2(wsel)
    oh, ol = _split2(buf[...])
    y = x1_ref[...] + (_dot(wh, oh) + _dot(wl, oh) + _dot(wh, ol))

    @pl.when(tile < n_first)
    def _():
        ya_ref[...] = y

    @pl.when(tile >= n_first)
    def _():
        yb_ref[...] = y


def _combine(cnt, glb, x1, gates, te, lr, o_sorted, tm, t_first):
    t, d = x1.shape
    n_first = t_first // tm
    grid_spec = pltpu.PrefetchScalarGridSpec(
        num_scalar_prefetch=2,
        grid=(t // tm,),
        in_specs=[pl.BlockSpec((tm, d), lambda i, *_: (i, 0)),
                  pl.BlockSpec((tm, LANES), lambda i, *_: (i, 0)),
                  pl.BlockSpec((tm, LANES), lambda i, *_: (i, 0)),
                  pl.BlockSpec((tm, LANES), lambda i, *_: (i, 0)),
                  pl.BlockSpec(memory_space=pl.ANY)],
        out_specs=list(_two_group_specs((tm, d), n_first)),
        scratch_shapes=[pltpu.VMEM((_local_rows(tm), d), F32), pltpu.SMEM((N_EXPERTS,), I32),
                        pltpu.SemaphoreType.DMA],
    )
    return pl.pallas_call(
        functools.partial(_combine_kernel, n_first=n_first),
        grid_spec=grid_spec,
        out_shape=[jax.ShapeDtypeStruct((t_first, d), F32), jax.ShapeDtypeStruct((t - t_first, d), F32)],
        compiler_params=_cparams(("arbitrary",)),
        name="moe_combine",
    )(cnt, glb, x1, gates, te, lr, o_sorted)


def _pick(n, pref):
    while n % pref:
        pref //= 2
    return pref


def _forward(x_prompt, x_sample, cache_k, cache_v, state_rec, norm_mix_w, w_in, q_norm_w, k_norm_w,
             attn_sinks, rec_lb_logits, rec_norm_w, w_attn_branch, w_rec_branch, w_out, norm_ffn_w,
             router_w, router_b, w_gate_up, b_gate_up, w_down, b_down, moe_bm=512, moe_tf=512):
    bp, sp, d = x_prompt.shape
    bs, ss, _ = x_sample.shape
    assert ss == CHUNK and sp % CHUNK == 0 and sp >= WINDOW
    tp, ts = bp * sp, bs * ss
    t = tp + ts
    xp2 = x_prompt.reshape(tp, d)
    xs2 = x_sample.reshape(ts, d)

    w = w_in[0]
    o_q, o_k, o_v, o_r = 0, Q_W, Q_W + KV_W, Q_W + 2 * KV_W
    o_g = o_r + 4 * REC_W
    w_perm = jnp.concatenate([w[:, o_g:], w[:, o_q:o_k], w[:, o_r:o_g], w[:, o_k:o_r]], axis=1).astype(BF16)
    c_aq = 2 * d
    c_rec = c_aq + Q_W
    c_k = c_rec + 4 * REC_W
    c_v = c_k + KV_W

    proj = _in_proj(xp2, xs2, norm_mix_w, w_perm, _pick(ts, 1024), 512)

    pos = jnp.concatenate([jnp.tile(jnp.arange(sp, dtype=I32), bp),
                           jnp.tile(PAST_LEN + jnp.arange(ss, dtype=I32), bs)]).astype(F32)
    half = HEAD_DIM // 2
    inv_freq = ROPE_THETA ** (-jnp.arange(half, dtype=F32) / half)
    ang = pos[:, None] * inv_freq[None, :]
    cos = jnp.tile(jnp.cos(ang), (1, LANES // half))
    sgn = jnp.tile(jnp.concatenate([-jnp.ones((half,), F32), jnp.ones((half,), F32)]), LANES // HEAD_DIM)
    sin = jnp.tile(jnp.sin(ang), (1, LANES // half)) * sgn[None, :]
    qw = jnp.tile(q_norm_w[0], LANES // HEAD_DIM)[None, :]
    kw = jnp.tile(k_norm_w[0], LANES // HEAD_DIM)[None, :]
    qr, kr = _qk_prep(proj, cos, sin, qw, kw, c_aq // Q_W, c_k // KV_W, _pick(ts, 512))

    sinks = attn_sinks[0]
    npc = sp // CHUNK
    vcol = c_v // KV_W

    def prow(back):
        return lambda b, c: b * npc + jnp.maximum(c - back, 0)

    oa_p = _attention(sinks, qr, [(kr, 0)] * 3, [(proj, vcol)] * 3, (bp, npc),
                      lambda b, c: b * npc + c, lambda b, c: b * npc + c, tp,
                      [prow(2), prow(1), prow(0)], True)
    ck = cache_k[0].reshape(bs * WINDOW, KV_W)
    cv = cache_v[0].reshape(bs * WINDOW, KV_W)
    srow = tp // CHUNK
    oa_s = _attention(sinks, qr, [(ck, 0), (ck, 0), (kr, 0)], [(cv, 0), (cv, 0), (proj, vcol)], (bs,),
                      lambda b: srow + b, lambda b: b, ts,
                      [lambda b: 2 * b, lambda b: 2 * b + 1, lambda b: srow + b], False)

    lb = jax.nn.softmax(rec_lb_logits.astype(F32), axis=0)[0][None, :]
    rec_cols = tuple(c_rec + i * REC_W for i in range(4))
    nw_rec = rec_norm_w[0][None, :]
    tb = _pick(sp, 512)
    or_p, sfin_p = _hgrn2(proj, lb, nw_rec, jnp.zeros((bp, N_REC_HEADS, REC_D, REC_D), F32),
                          0, bp, sp, tb, rec_cols, tp)
    or_s, sfin_s = _hgrn2(proj, lb, nw_rec, state_rec[0], tp // CHUNK, bs, ss, CHUNK, rec_cols, ts)

    rw = jnp.pad(router_w[0], ((0, 0), (0, LANES - N_EXPERTS)))
    rwh = rw.astype(BF16)
    rwl = (rw - rwh.astype(F32)).astype(BF16)
    rb = jnp.concatenate([router_b[0].astype(F32), jnp.full((LANES - N_EXPERTS,), -jnp.inf, F32)])[None, :]
    tm = _pick(ts, 256)
    x1, h2, gt, te, lr, tcnt, tpre = _merge(xp2, xs2, oa_p, oa_s, or_p, or_s, proj, 0, 1,
                                            w_attn_branch[0].astype(BF16),
                                            w_rec_branch[0].astype(BF16), w_out[0].astype(BF16),
                                            norm_ffn_w, rwh, rwl, rb, tm)

    tcnt = tcnt[:, 0, :N_EXPERTS]
    counts = jnp.sum(tcnt, axis=0)
    padded = (counts + moe_bm - 1) // moe_bm * moe_bm
    pad_end = jnp.cumsum(padded)
    pad_start = pad_end - padded
    nb = (t * TOP_K) // moe_bm + N_EXPERTS
    block_start = jnp.arange(nb, dtype=I32) * moe_bm
    block_e = jnp.minimum(jnp.sum((pad_end[None, :] <= block_start[:, None]).astype(I32), axis=1), N_EXPERTS - 1)
    n_active = (pad_end[-1:] // moe_bm).astype(I32)
    run_cnt = tcnt.reshape(-1)
    run_glb = (pad_start[None, :] + tpre[:, 0, :N_EXPERTS]).reshape(-1)

    xs = _dispatch(run_cnt, run_glb, h2, te, lr, nb * moe_bm, tm)
    o_sorted = _moe_blocks(block_e, n_active, xs, w_gate_up[0], b_gate_up[0], w_down[0], b_down[0],
                           moe_bm, moe_tf)
    y_p, y_s = _combine(run_cnt, run_glb, x1, gt, te, lr, o_sorted, tm, tp)
    y_p = y_p.reshape(bp, sp, d)
    y_s = y_s.reshape(bs, ss, d)
    v_new = proj[:, c_v:c_v + KV_W]
    kp = kr[:tp].reshape(bp, sp, N_KV_HEADS, HEAD_DIM)[:, -WINDOW:]
    vp = v_new[:tp].reshape(bp, sp, N_KV_HEADS, HEAD_DIM)[:, -WINDOW:]
    ks_new = kr[tp:].reshape(bs, ss, N_KV_HEADS, HEAD_DIM)
    vs_new = v_new[tp:].reshape(bs, ss, N_KV_HEADS, HEAD_DIM)
    ks = jnp.concatenate([cache_k[0], ks_new], axis=1)[:, -WINDOW:]
    vs = jnp.concatenate([cache_v[0], vs_new], axis=1)[:, -WINDOW:]
    return (y_p, y_s, kp[None], vp[None], sfin_p[None], ks[None], vs[None], sfin_s[None])


def kernel(x_prompt, x_sample, cache_k, cache_v, state_rec, norm_mix_w, w_in, q_norm_w, k_norm_w, attn_sinks, rec_lb_logits, rec_norm_w, w_attn_branch, w_rec_branch, w_out, norm_ffn_w, router_w, router_b, w_gate_up, b_gate_up, w_down, b_down):
    return _forward(x_prompt, x_sample, cache_k, cache_v, state_rec, norm_mix_w, w_in, q_norm_w, k_norm_w,
                    attn_sinks, rec_lb_logits, rec_norm_w, w_attn_branch, w_rec_branch, w_out, norm_ffn_w,
                    router_w, router_b, w_gate_up, b_gate_up, w_down, b_down)
```

```python
import functools

import jax
import jax.numpy as jnp
from jax import lax
from jax.experimental import pallas as pl
from jax.experimental.pallas import tpu as pltpu

F32 = jnp.float32
BF16 = jnp.bfloat16
I32 = jnp.int32

CHUNK = 64
N_HEADS = 16
N_KV_HEADS = 4
HEAD_DIM = 64
WINDOW = 128
ROPE_THETA = 10000.0
N_REC_HEADS = 8
REC_D = 128
N_EXPERTS = 32
TOP_K = 4
SWIGLU_LIMIT = 7.0
SWIGLU_ALPHA = 1.702
EPS = 1e-6
PAST_LEN = 2048
LANES = 128
SUB = 16

VMEM_LIMIT = 56 * 1024 * 1024

Q_W = N_HEADS * HEAD_DIM
KV_W = N_KV_HEADS * HEAD_DIM
REC_W = N_REC_HEADS * REC_D


def _cparams(sem):
    return pltpu.CompilerParams(dimension_semantics=sem, vmem_limit_bytes=VMEM_LIMIT)


def _sigmoid(x):
    return 1.0 / (1.0 + jnp.exp(-x))


def _split2(x):
    hi = x.astype(BF16)
    lo = (x - hi.astype(F32)).astype(BF16)
    return hi, lo


def _dot(a, b):
    return jnp.dot(a, b, preferred_element_type=F32)


def _dot_nt(a, b):
    return lax.dot_general(a, b, (((1,), (1,)), ((), ())), preferred_element_type=F32)


def _two_group_specs(shape, n_first):
    first = pl.BlockSpec(shape, lambda i, *_: (jnp.minimum(i, n_first - 1), 0))
    second = pl.BlockSpec(shape, lambda i, *_: (jnp.maximum(i - n_first, 0), 0))
    return first, second


def _in_proj_kernel(xa_ref, xb_ref, nw_ref, w_ref, o_ref, h_scr, *, n_first):
    def norm(x_ref):
        x = x_ref[...]
        ms = jnp.mean(x * x, axis=-1, keepdims=True)
        h_scr[...] = (x * lax.rsqrt(ms + EPS) * nw_ref[...]).astype(BF16)

    first_col = pl.program_id(1) == 0
    in_first = pl.program_id(0) < n_first
    pl.when(jnp.logical_and(first_col, in_first))(lambda: norm(xa_ref))
    pl.when(jnp.logical_and(first_col, jnp.logical_not(in_first)))(lambda: norm(xb_ref))
    o_ref[...] = _dot(h_scr[...], w_ref[...])


def _in_proj(xa, xb, norm_w, w, tm, tn):
    d = xa.shape[1]
    t = xa.shape[0] + xb.shape[0]
    n = w.shape[1]
    n_first = xa.shape[0] // tm
    return pl.pallas_call(
        functools.partial(_in_proj_kernel, n_first=n_first),
        grid=(t // tm, n // tn),
        in_specs=[*_two_group_specs((tm, d), n_first),
                  pl.BlockSpec((1, d), lambda i, j: (0, 0)),
                  pl.BlockSpec((d, tn), lambda i, j: (0, j))],
        out_specs=pl.BlockSpec((tm, tn), lambda i, j: (i, j)),
        out_shape=jax.ShapeDtypeStruct((t, n), F32),
        scratch_shapes=[pltpu.VMEM((tm, d), BF16)],
        compiler_params=_cparams(("parallel", "arbitrary")),
        name="in_proj",
    )(xa, xb, norm_w, w)


def _norm_rope(x, nw, cos, sin, bd):
    hi, lo = _split2(x * x)
    ss = _dot(hi, bd) + _dot(lo, bd)
    y = x * lax.rsqrt(ss * (1.0 / HEAD_DIM) + EPS) * nw
    lane = lax.broadcasted_iota(I32, y.shape, 1)
    first_half = (lane % HEAD_DIM) < (HEAD_DIM // 2)
    rot = jnp.where(first_half, pltpu.roll(y, LANES - HEAD_DIM // 2, 1), pltpu.roll(y, HEAD_DIM // 2, 1))
    return y * cos + rot * sin


def _qk_prep_kernel(q_ref, k_ref, cos_ref, sin_ref, qw_ref, kw_ref, qo_ref, ko_ref):
    r = lax.broadcasted_iota(I32, (LANES, LANES), 0) // HEAD_DIM
    c = lax.broadcasted_iota(I32, (LANES, LANES), 1) // HEAD_DIM
    bd = jnp.where(r == c, 1.0, 0.0).astype(BF16)
    cos = cos_ref[...]
    sin = sin_ref[...]
    scale = HEAD_DIM ** -0.5
    for g in range(Q_W // LANES):
        sl = slice(g * LANES, (g + 1) * LANES)
        qo_ref[:, sl] = (_norm_rope(q_ref[:, sl], qw_ref[...], cos, sin, bd) * scale).astype(BF16)
    for g in range(KV_W // LANES):
        sl = slice(g * LANES, (g + 1) * LANES)
        ko_ref[:, sl] = _norm_rope(k_ref[:, sl], kw_ref[...], cos, sin, bd)


def _qk_prep(proj, cos, sin, qw, kw, q_blk, k_blk, tr):
    t = proj.shape[0]
    return pl.pallas_call(
        _qk_prep_kernel,
        grid=(t // tr,),
        in_specs=[pl.BlockSpec((tr, Q_W), lambda i: (i, q_blk)),
                  pl.BlockSpec((tr, KV_W), lambda i: (i, k_blk)),
                  pl.BlockSpec((tr, LANES), lambda i: (i, 0)),
                  pl.BlockSpec((tr, LANES), lambda i: (i, 0)),
                  pl.BlockSpec((1, LANES), lambda i: (0, 0)),
                  pl.BlockSpec((1, LANES), lambda i: (0, 0))],
        out_specs=[pl.BlockSpec((tr, Q_W), lambda i: (i, 0)),
                   pl.BlockSpec((tr, KV_W), lambda i: (i, 0))],
        out_shape=[jax.ShapeDtypeStruct((t, Q_W), BF16), jax.ShapeDtypeStruct((t, KV_W), F32)],
        compiler_params=_cparams(("parallel",)),
        name="qk_prep",
    )(proj, proj, cos, sin, qw, kw)


def _attn_kernel(sink_ref, q_ref, *refs, masked, n_kv, n_sub):
    k_all = jnp.concatenate([r[...] for r in refs[:n_kv]], axis=0)
    v_all = jnp.concatenate([r[...] for r in refs[n_kv:2 * n_kv]], axis=0)
    o_ref = refs[2 * n_kv]
    n_keys = 3 * CHUNK
    lane = lax.broadcasted_iota(I32, (n_keys, LANES), 1)
    low = lane < HEAD_DIM
    row = lax.broadcasted_iota(I32, (2 * CHUNK, 1), 0)
    for u in range(n_sub):
        rows = slice(u * CHUNK, (u + 1) * CHUNK)
        k_u = k_all[u * CHUNK:u * CHUNK + n_keys]
        v_u = v_all[u * CHUNK:u * CHUNK + n_keys]
        if masked:
            first = n_sub * pl.program_id(1) + u
            key_chunk = lax.broadcasted_iota(I32, (2 * CHUNK, n_keys), 1) // CHUNK
            valid = (key_chunk + first) >= 2
        for j in range(N_KV_HEADS):
            pair = slice((j // 2) * LANES, (j // 2 + 1) * LANES)
            kp = k_u[:, pair]
            vp = v_u[:, pair]
            kr = pltpu.roll(kp, HEAD_DIM, 1)
            vr = pltpu.roll(vp, HEAD_DIM, 1)
            if j % 2 == 0:
                k_low, k_high, v_low, v_high = kp, kr, vp, vr
            else:
                k_low, k_high, v_low, v_high = kr, kp, vr, vp
            q2 = jnp.concatenate([q_ref[rows, (2 * j) * LANES:(2 * j + 1) * LANES],
                                  q_ref[rows, (2 * j + 1) * LANES:(2 * j + 2) * LANES]], axis=0)
            o = jnp.zeros((2 * CHUNK, LANES), F32)
            for half, (kk, vv) in enumerate(((k_low, v_low), (k_high, v_high))):
                keep = low if half == 0 else jnp.logical_not(low)
                kz = jnp.where(keep, kk, 0.0).astype(BF16)
                vz = jnp.where(keep, vv, 0.0).astype(BF16)
                s = _dot_nt(q2, kz)
                if masked:
                    s = jnp.where(valid, s, -jnp.inf)
                sink = jnp.where(row < CHUNK, sink_ref[4 * j + half], sink_ref[4 * j + 2 + half])
                m = jnp.maximum(jnp.max(s, axis=-1, keepdims=True), sink)
                p = jnp.exp(s - m)
                p = p / (jnp.sum(p, axis=-1, keepdims=True) + jnp.exp(sink - m))
                o = o + _dot(p.astype(BF16), vz)
            o_ref[rows, (2 * j) * LANES:(2 * j + 1) * LANES] = o[:CHUNK].astype(BF16)
            o_ref[rows, (2 * j + 1) * LANES:(2 * j + 2) * LANES] = o[CHUNK:].astype(BF16)


def _attention(sinks, qr, k_srcs, v_srcs, grid, q_map, o_map, n_rows, kv_maps, kv_rows, n_sub, masked):
    assert kv_rows * len(k_srcs) == (n_sub + 2) * CHUNK

    def spec(col, fn):
        return pl.BlockSpec((kv_rows, KV_W), lambda *g: (fn(*g), col))

    in_specs = [pl.BlockSpec(memory_space=pltpu.SMEM),
                pl.BlockSpec((n_sub * CHUNK, Q_W), lambda *g: (q_map(*g), 0))]
    in_specs += [spec(col, fn) for (_, col), fn in zip(k_srcs, kv_maps)]
    in_specs += [spec(col, fn) for (_, col), fn in zip(v_srcs, kv_maps)]
    return pl.pallas_call(
        functools.partial(_attn_kernel, masked=masked, n_kv=len(k_srcs), n_sub=n_sub),
        grid=grid,
        in_specs=in_specs,
        out_specs=pl.BlockSpec((n_sub * CHUNK, Q_W), lambda *g: (o_map(*g), 0)),
        out_shape=jax.ShapeDtypeStruct((n_rows, Q_W), BF16),
        compiler_params=_cparams(("parallel",) * len(grid)),
        name="attn_prompt" if masked else "attn_sample",
    )(sinks, qr, *[a for a, _ in k_srcs], *[a for a, _ in v_srcs])


HEADS_PER_STEP = 8


def _hgrn2_chunk(r0, sl, hh, rq_ref, rf_ref, rv_ref, rg_ref, lb_ref, nw_ref, o_ref,
                 st_scr, q_scr, k_scr, v_scr, c_scr, tri, ones, rsub):
    n_sub = CHUNK // SUB
    rq = rq_ref[pl.ds(r0, CHUNK), sl]
    q = rq * _sigmoid(rq)
    k = (1.0 - lb_ref[:, sl]) * _sigmoid(-rf_ref[pl.ds(r0, CHUNK), sl])
    g = jnp.log1p(-k)
    v = rv_ref[pl.ds(r0, CHUNK), sl]
    g1 = g.astype(BF16)
    e1 = g - g1.astype(F32)
    g2 = e1.astype(BF16)
    g3 = (e1 - g2.astype(F32)).astype(BF16)
    cum = _dot(tri, g1) + _dot(tri, g2) + _dot(tri, g3)
    q_scr[hh] = q
    k_scr[hh] = k
    v_scr[hh] = v
    c_scr[hh] = cum
    st = st_scr[hh]
    inter = _dot_nt((q * jnp.exp(cum)).astype(BF16), st.astype(BF16))
    outs = []
    for i in range(n_sub):
        lo = i * SUB
        q_i = q_scr[hh, lo:lo + SUB, :]
        c_i = c_scr[hh, lo:lo + SUB, :]
        acc = inter[lo:lo + SUB, :]
        if i > 0:
            ref = c_scr[hh, lo - 1:lo, :]
            qd = (q_i * jnp.exp(c_i - ref)).astype(BF16)
            kd = (k_scr[hh, 0:lo, :] * jnp.exp(ref - c_scr[hh, 0:lo, :])).astype(BF16)
            s = _dot_nt(qd, kd)
            acc = acc + _dot(s.astype(BF16), v_scr[hh, 0:lo, :].astype(BF16))
        parts = []
        for s_ in range(SUB):
            arg = jnp.where(rsub >= s_, c_i - c_scr[hh, lo + s_:lo + s_ + 1, :], -jnp.inf)
            parts.append(q_i * k_scr[hh, lo + s_:lo + s_ + 1, :] * jnp.exp(arg))
        dsum = _dot(jnp.concatenate(parts, axis=0).astype(BF16), ones)
        for s_ in range(SUB):
            acc = acc + dsum[s_ * SUB:(s_ + 1) * SUB, :] * v_scr[hh, lo + s_:lo + s_ + 1, :]
        outs.append(acc)
    o = jnp.concatenate(outs, axis=0)
    last = c_scr[hh, CHUNK - 1:CHUNK, :]
    kdec = (k * jnp.exp(last - cum)).astype(BF16)
    st_scr[hh] = st * jnp.exp(last) + _dot(v.T.astype(BF16), kdec)
    ms = jnp.mean(o * o, axis=-1, keepdims=True)
    rg = rg_ref[pl.ds(r0, CHUNK), sl]
    o_ref[pl.ds(r0, CHUNK), sl] = (o * lax.rsqrt(ms + EPS) * nw_ref[...] * (rg * _sigmoid(rg))).astype(BF16)


def _hgrn2_kernel(rq_ref, rf_ref, rv_ref, rg_ref, lb_ref, nw_ref, s0_ref, o_ref, sfin_ref,
                  st_scr, q_scr, k_scr, v_scr, c_scr, *, n_chunks):
    t = pl.program_id(2)

    @pl.when(t == 0)
    def _():
        for hh in range(HEADS_PER_STEP):
            st_scr[hh] = s0_ref[0, hh].T

    r64 = lax.broadcasted_iota(I32, (CHUNK, CHUNK), 0)
    c64 = lax.broadcasted_iota(I32, (CHUNK, CHUNK), 1)
    tri = jnp.where(r64 >= c64, 1.0, 0.0).astype(BF16)
    ones = jnp.ones((LANES, LANES), BF16)
    rsub = lax.broadcasted_iota(I32, (SUB, LANES), 0)

    def chunk(ci, carry):
        r0 = pl.multiple_of(ci * CHUNK, CHUNK)
        for hh in range(HEADS_PER_STEP):
            sl = slice(hh * REC_D, (hh + 1) * REC_D)
            _hgrn2_chunk(r0, sl, hh, rq_ref, rf_ref, rv_ref, rg_ref, lb_ref, nw_ref, o_ref,
                         st_scr, q_scr, k_scr, v_scr, c_scr, tri, ones, rsub)
        return carry

    lax.fori_loop(0, n_chunks, chunk, 0)

    @pl.when(t == pl.num_programs(2) - 1)
    def _():
        for hh in range(HEADS_PER_STEP):
            sfin_ref[0, hh] = st_scr[hh].T


def _hgrn2(proj, lb, norm_w, s0, row_blk0, n_seq, seq_len, tb, col_blks, out_rows):
    nt = seq_len // tb
    hb = HEADS_PER_STEP
    wb = hb * REC_D
    cq, cf, cv, cg = col_blks

    def in_spec(c0):
        return pl.BlockSpec((tb, wb), lambda b, h, t: (row_blk0 + b * nt + t, c0 // wb + h))

    return pl.pallas_call(
        functools.partial(_hgrn2_kernel, n_chunks=tb // CHUNK),
        grid=(n_seq, N_REC_HEADS // hb, nt),
        in_specs=[in_spec(cq), in_spec(cf), in_spec(cv), in_spec(cg),
                  pl.BlockSpec((1, wb), lambda b, h, t: (0, h)),
                  pl.BlockSpec((1, REC_D), lambda b, h, t: (0, 0)),
                  pl.BlockSpec((1, hb, REC_D, REC_D), lambda b, h, t: (b, h, 0, 0))],
        out_specs=[pl.BlockSpec((tb, wb), lambda b, h, t: (b * nt + t, h)),
                   pl.BlockSpec((1, hb, REC_D, REC_D), lambda b, h, t: (b, h, 0, 0))],
        out_shape=[jax.ShapeDtypeStruct((out_rows, REC_W), BF16),
                   jax.ShapeDtypeStruct((n_seq, N_REC_HEADS, REC_D, REC_D), F32)],
        scratch_shapes=[pltpu.VMEM((hb, REC_D, REC_D), F32)] + [pltpu.VMEM((hb, CHUNK, REC_D), F32)] * 4,
        compiler_params=_cparams(("parallel", "parallel", "arbitrary")),
        name="hgrn2",
    )(proj, proj, proj, proj, lb, norm_w, s0)


def _pack_bf16_pairs(h):
    d = h.shape[1]
    bits = lax.bitcast_convert_type(h, jnp.uint32)
    return (bits[:, d // 2:] & jnp.uint32(0xFFFF0000)) | (bits[:, :d // 2] >> 16)


def _merge_kernel(xa_ref, xb_ref, oaa_ref, oab_ref, ora_ref, orb_ref, ga_ref, gr_ref, wa_ref, wr_ref, wo_ref,
                  nf_ref, rwh_ref, rwl_ref, rb_ref,
                  x1_ref, h2_ref, gt_ref, te_ref, lr_ref, tcnt_ref, tpre_ref, carry_scr, *, n_first):
    i = pl.program_id(0)

    @pl.when(i == 0)
    def _():
        carry_scr[...] = jnp.zeros_like(carry_scr)

    in_first = i < n_first
    a = _dot(jnp.where(in_first, oaa_ref[...], oab_ref[...]), wa_ref[...])
    r = _dot(jnp.where(in_first, ora_ref[...], orb_ref[...]), wr_ref[...])
    merged = _sigmoid(ga_ref[...]) * a + _sigmoid(gr_ref[...]) * r
    x1 = jnp.where(in_first, xa_ref[...], xb_ref[...]) + _dot(merged.astype(BF16), wo_ref[...])
    x1_ref[...] = x1
    ms = jnp.mean(x1 * x1, axis=-1, keepdims=True)
    h2 = x1 * lax.rsqrt(ms + EPS) * nf_ref[...]
    tm = h2.shape[0]
    hh, hl = _split2(h2)
    logits = _dot(hh, rwh_ref[...]) + _dot(hl, rwh_ref[...]) + _dot(hh, rwl_ref[...]) + rb_ref[...]
    lane = lax.broadcasted_iota(I32, (tm, LANES), 1)
    lane_f = lane.astype(F32)
    work = logits
    h2_ref[...] = hh
    te = jnp.zeros((tm, LANES), I32)
    tv = jnp.zeros((tm, LANES), F32)
    onehot = jnp.zeros((tm, LANES), F32)
    sels = []
    for k in range(TOP_K):
        m = jnp.max(work, axis=-1, keepdims=True)
        idx = jnp.min(jnp.where(work == m, lane_f, float(LANES)), axis=-1, keepdims=True).astype(I32)
        sel = lane == idx
        work = jnp.where(sel, -jnp.inf, work)
        onehot = onehot + jnp.where(sel, 1.0, 0.0)
        te = jnp.where(lane == k, idx, te)
        tv = jnp.where(lane == k, m, tv)
        sels.append(sel)
    ex = jnp.where(lane < TOP_K, jnp.exp(tv - tv[:, 0:1]), 0.0)
    gt_ref[...] = ex / jnp.sum(ex, axis=-1, keepdims=True)
    te_ref[...] = te
    rr = lax.broadcasted_iota(I32, (tm, tm), 0)
    cc = lax.broadcasted_iota(I32, (tm, tm), 1)
    strict = jnp.where(rr > cc, 1.0, 0.0).astype(BF16)
    lrank = _dot(strict, onehot.astype(BF16))
    lr = jnp.zeros((tm, LANES), F32)
    for k in range(TOP_K):
        lr = jnp.where(lane == k, jnp.sum(jnp.where(sels[k], lrank, 0.0), axis=-1, keepdims=True), lr)
    lr_ref[...] = lr.astype(I32)
    tcnt = jnp.sum(onehot, axis=0, keepdims=True)
    tcnt_ref[0] = tcnt.astype(I32)
    tpre_ref[0] = carry_scr[...].astype(I32)
    carry_scr[...] = carry_scr[...] + tcnt


def _merge(xa, xb, oaa, oab, ora, orb, proj, ga_blk, gr_blk, wa, wr, wo, nf, rwh, rwl, rb, tm):
    d = xa.shape[1]
    t = xa.shape[0] + xb.shape[0]
    nt = t // tm
    n_first = xa.shape[0] // tm

    def const(shape):
        return pl.BlockSpec(shape, lambda i: (0,) * len(shape), pipeline_mode=pl.Buffered(1))

    tile_vec = pl.BlockSpec((1, 1, LANES), lambda i: (i, 0, 0))
    return pl.pallas_call(
        functools.partial(_merge_kernel, n_first=n_first),
        grid=(nt,),
        in_specs=[*_two_group_specs((tm, d), n_first),
                  *_two_group_specs((tm, Q_W), n_first),
                  *_two_group_specs((tm, REC_W), n_first),
                  pl.BlockSpec((tm, d), lambda i: (i, ga_blk)),
                  pl.BlockSpec((tm, d), lambda i: (i, gr_blk)),
                  const((Q_W, d)), const((REC_W, d)), const((d, d)), const((1, d)),
                  const((d, LANES)), const((d, LANES)), const((1, LANES))],
        out_specs=[pl.BlockSpec((tm, d), lambda i: (i, 0)),
                   pl.BlockSpec((tm, d), lambda i: (i, 0)),
                   pl.BlockSpec((tm, LANES), lambda i: (i, 0)),
                   pl.BlockSpec((tm, LANES), lambda i: (i, 0)),
                   pl.BlockSpec((tm, LANES), lambda i: (i, 0)),
                   tile_vec, tile_vec],
        out_shape=[jax.ShapeDtypeStruct((t, d), F32), jax.ShapeDtypeStruct((t, d), BF16),
                   jax.ShapeDtypeStruct((t, LANES), F32), jax.ShapeDtypeStruct((t, LANES), I32),
                   jax.ShapeDtypeStruct((t, LANES), I32),
                   jax.ShapeDtypeStruct((nt, 1, LANES), I32), jax.ShapeDtypeStruct((nt, 1, LANES), I32)],
        scratch_shapes=[pltpu.VMEM((1, LANES), F32)],
        compiler_params=_cparams(("arbitrary",)),
        name="merge_router",
    )(xa, xb, oaa, oab, ora, orb, proj, proj, wa, wr, wo, nf, rwh, rwl, rb)


SUBLANES = 8
RUN_BITS = 9


def _local_rows(tm):
    return TOP_K * tm + SUBLANES * N_EXPERTS


def _run_layout(cnt_ref, glb_ref, tile, loc_scr):
    def body(e, nxt):
        j = tile * N_EXPERTS + e
        loc = nxt + ((glb_ref[j] - nxt) & (SUBLANES - 1))
        loc_scr[e] = loc
        return loc + cnt_ref[j]

    lax.fori_loop(0, N_EXPERTS, body, 0)


def _run_copies(cnt_ref, glb_ref, loc_scr, tile, make_copy, action):
    def per_expert(e, c):
        j = tile * N_EXPERTS + e
        n = cnt_ref[j]
        glb = glb_ref[j]
        loc = loc_scr[e]
        head = jnp.minimum(n, (-glb) & (SUBLANES - 1))
        mid = ((n - head) // SUBLANES) * SUBLANES

        def single(i, c2):
            action(make_copy(loc + i, glb + i, 1))
            return c2

        lax.fori_loop(0, head, single, 0)
        for bit in range(RUN_BITS - 1, 2, -1):
            size = 1 << bit

            @pl.when((mid & size) != 0)
            def _():
                done = head + ((mid >> (bit + 1)) << (bit + 1))
                action(make_copy(pl.multiple_of(loc + done, SUBLANES), pl.multiple_of(glb + done, SUBLANES), size))

        lax.fori_loop(head + mid, n, single, 0)
        return c

    lax.fori_loop(0, N_EXPERTS, per_expert, 0)


def _positions(te, lr, loc_scr, n_local):
    tm = te.shape[0]
    lane = lax.broadcasted_iota(I32, (1, LANES), 1)
    lvec = jnp.zeros((1, LANES), I32)
    for e in range(N_EXPERTS):
        lvec = jnp.where(lane == e, loc_scr[e], lvec)
    lvec = lvec.astype(F32)
    lane_t = lax.broadcasted_iota(I32, (tm, LANES), 1)
    pos = []
    for k in range(TOP_K):
        start = jnp.sum(jnp.where(lane_t == te[:, k:k + 1], lvec, 0.0), axis=-1, keepdims=True)
        pos.append(start.astype(I32) + lr[:, k:k + 1])
    return pos, lax.broadcasted_iota(I32, (tm, n_local), 1)


def _dispatch_kernel(cnt_ref, glb_ref, h_ref, te_ref, lr_ref, xs_in_ref, xs_ref, srt_scr, loc_scr, sem):
    del xs_in_ref
    tile = pl.program_id(0)
    _run_layout(cnt_ref, glb_ref, tile, loc_scr)
    pos, slot_iota = _positions(te_ref[...], lr_ref[...], loc_scr, srt_scr.shape[0])
    perm_t = jnp.zeros(slot_iota.shape, F32)
    for k in range(TOP_K):
        perm_t = perm_t + jnp.where(slot_iota == pos[k], 1.0, 0.0)
    rows = lax.dot_general(perm_t.astype(BF16), h_ref[...], (((0,), (0,)), ((), ())), preferred_element_type=F32)
    srt_scr[...] = _pack_bf16_pairs(rows)

    def make_copy(loc, glb, size):
        return pltpu.make_async_copy(srt_scr.at[pl.ds(loc, size)], xs_ref.at[pl.ds(glb, size)], sem)

    _run_copies(cnt_ref, glb_ref, loc_scr, tile, make_copy, lambda cp: cp.start())
    _run_copies(cnt_ref, glb_ref, loc_scr, tile, make_copy, lambda cp: cp.wait())


def _dispatch(cnt, glb, h2, te, lr, n_rows, tm):
    t, d = h2.shape
    assert tm < (1 << RUN_BITS)
    xs0 = jnp.zeros((n_rows, d // 2), jnp.uint32)
    grid_spec = pltpu.PrefetchScalarGridSpec(
        num_scalar_prefetch=2,
        grid=(t // tm,),
        in_specs=[pl.BlockSpec((tm, d), lambda i, *_: (i, 0)),
                  pl.BlockSpec((tm, LANES), lambda i, *_: (i, 0)),
                  pl.BlockSpec((tm, LANES), lambda i, *_: (i, 0)),
                  pl.BlockSpec(memory_space=pl.ANY)],
        out_specs=pl.BlockSpec(memory_space=pl.ANY),
        scratch_shapes=[pltpu.VMEM((_local_rows(tm), d // 2), jnp.uint32), pltpu.SMEM((N_EXPERTS,), I32),
                        pltpu.SemaphoreType.DMA],
    )
    return pl.pallas_call(
        _dispatch_kernel,
        grid_spec=grid_spec,
        out_shape=jax.ShapeDtypeStruct(xs0.shape, xs0.dtype),
        input_output_aliases={5: 0},
        compiler_params=_cparams(("arbitrary",)),
        name="moe_dispatch",
    )(cnt, glb, h2, te, lr, xs0)


def _moe_kernel(be_ref, na_ref, nv_ref, x_ref, wg_ref, wu_ref, wd_ref, bg_ref, bu_ref, bd_ref, o_ref, xb_scr):
    b = pl.program_id(0)
    f = pl.program_id(1)
    bm = o_ref.shape[0]
    half = x_ref.shape[1]

    def step(rows):
        @pl.when(f == 0)
        def _():
            pk = x_ref[0:rows, :]
            xb_scr[0:rows, :half] = lax.bitcast_convert_type(pk << 16, F32).astype(BF16)
            xb_scr[0:rows, half:] = lax.bitcast_convert_type(pk & jnp.uint32(0xFFFF0000), F32).astype(BF16)
            o_ref[0:rows, :] = jnp.broadcast_to(bd_ref[...], (rows, o_ref.shape[1]))
            if rows < bm:
                o_ref[rows:, :] = jnp.zeros((bm - rows, o_ref.shape[1]), F32)

        xb = xb_scr[0:rows, :]
        g = _dot(xb, wg_ref[...].astype(BF16)) + bg_ref[...]
        u = _dot(xb, wu_ref[...].astype(BF16)) + bu_ref[...]
        g = jnp.minimum(g, SWIGLU_LIMIT)
        u = jnp.clip(u, -SWIGLU_LIMIT, SWIGLU_LIMIT)
        act = (u + 1.0) * (g * _sigmoid(SWIGLU_ALPHA * g))
        o_ref[0:rows, :] += _dot(act.astype(BF16), wd_ref[...].astype(BF16))

    active = b < na_ref[0]
    full = nv_ref[b] > bm // 2
    pl.when(jnp.logical_and(active, full))(lambda: step(bm))
    pl.when(jnp.logical_and(active, jnp.logical_not(full)))(lambda: step(bm // 2))

    @pl.when(jnp.logical_and(jnp.logical_not(active), f == 0))
    def _():
        o_ref[...] = jnp.zeros_like(o_ref)


def _moe_blocks(block_e, n_active, n_valid, xs, w_gate_up, b_gate_up, w_down, b_down, bm, tf):
    n_exp, d, two_f = w_gate_up.shape
    n_rows = xs.shape[0]
    d_ff = two_f // 2
    nb = n_rows // bm
    nf = d_ff // tf

    def blk(b, na):
        return jnp.minimum(b, na[0] - 1)

    def fi(b, f, na):
        return jnp.where(b < na[0], f, nf - 1)

    grid_spec = pltpu.PrefetchScalarGridSpec(
        num_scalar_prefetch=3,
        grid=(nb, nf),
        in_specs=[pl.BlockSpec((bm, d // 2), lambda b, f, be, na, nv: (blk(b, na), 0)),
                  pl.BlockSpec((None, d, tf), lambda b, f, be, na, nv: (be[blk(b, na)], 0, fi(b, f, na))),
                  pl.BlockSpec((None, d, tf), lambda b, f, be, na, nv: (be[blk(b, na)], 0, fi(b, f, na) + nf)),
                  pl.BlockSpec((None, tf, d), lambda b, f, be, na, nv: (be[blk(b, na)], fi(b, f, na), 0)),
                  pl.BlockSpec((None, 1, tf), lambda b, f, be, na, nv: (be[blk(b, na)], 0, fi(b, f, na))),
                  pl.BlockSpec((None, 1, tf), lambda b, f, be, na, nv: (be[blk(b, na)], 0, fi(b, f, na) + nf)),
                  pl.BlockSpec((None, 1, d), lambda b, f, be, na, nv: (be[blk(b, na)], 0, 0))],
        out_specs=pl.BlockSpec((bm, d), lambda b, f, be, na, nv: (b, 0)),
        scratch_shapes=[pltpu.VMEM((bm, d), BF16)],
    )
    return pl.pallas_call(
        _moe_kernel,
        grid_spec=grid_spec,
        out_shape=jax.ShapeDtypeStruct((n_rows, d), F32),
        compiler_params=_cparams(("arbitrary", "arbitrary")),
        name="moe_experts",
    )(block_e, n_active, n_valid, xs, w_gate_up, w_gate_up, w_down,
      b_gate_up.reshape(n_exp, 1, two_f), b_gate_up.reshape(n_exp, 1, two_f), b_down.reshape(n_exp, 1, d))


def _combine_kernel(cnt_ref, glb_ref, x1_ref, gt_ref, te_ref, lr_ref, os_ref, ya_ref, yb_ref, buf, loc_scr, sem, *,
                    n_first):
    tile = pl.program_id(0)

    @pl.when(tile == 0)
    def _():
        buf[...] = jnp.zeros_like(buf)

    _run_layout(cnt_ref, glb_ref, tile, loc_scr)

    def make_copy(loc, glb, size):
        return pltpu.make_async_copy(os_ref.at[pl.ds(glb, size)], buf.at[pl.ds(loc, size)], sem)

    _run_copies(cnt_ref, glb_ref, loc_scr, tile, make_copy, lambda cp: cp.start())
    pos, slot_iota = _positions(te_ref[...], lr_ref[...], loc_scr, buf.shape[0])
    gt = gt_ref[...]
    wsel = jnp.zeros(slot_iota.shape, F32)
    for k in range(TOP_K):
        wsel = wsel + jnp.where(slot_iota == pos[k], gt[:, k:k + 1], 0.0)
    _run_copies(cnt_ref, glb_ref, loc_scr, tile, make_copy, lambda cp: cp.wait())
    wh, wl = _split2(wsel)
    oh, ol = _split2(buf[...])
    y = x1_ref[...] + (_dot(wh, oh) + _dot(wl, oh) + _dot(wh, ol))

    @pl.when(tile < n_first)
    def _():
        ya_ref[...] = y

    @pl.when(tile >= n_first)
    def _():
        yb_ref[...] = y


def _combine(cnt, glb, x1, gates, te, lr, o_sorted, tm, t_first):
    t, d = x1.shape
    n_first = t_first // tm
    grid_spec = pltpu.PrefetchScalarGridSpec(
        num_scalar_prefetch=2,
        grid=(t // tm,),
        in_specs=[pl.BlockSpec((tm, d), lambda i, *_: (i, 0)),
                  pl.BlockSpec((tm, LANES), lambda i, *_: (i, 0)),
                  pl.BlockSpec((tm, LANES), lambda i, *_: (i, 0)),
                  pl.BlockSpec((tm, LANES), lambda i, *_: (i, 0)),
                  pl.BlockSpec(memory_space=pl.ANY)],
        out_specs=list(_two_group_specs((tm, d), n_first)),
        scratch_shapes=[pltpu.VMEM((_local_rows(tm), d), F32), pltpu.SMEM((N_EXPERTS,), I32),
                        pltpu.SemaphoreType.DMA],
    )
    return pl.pallas_call(
        functools.partial(_combine_kernel, n_first=n_first),
        grid_spec=grid_spec,
        out_shape=[jax.ShapeDtypeStruct((t_first, d), F32), jax.ShapeDtypeStruct((t - t_first, d), F32)],
        compiler_params=_cparams(("arbitrary",)),
        name="moe_combine",
    )(cnt, glb, x1, gates, te, lr, o_sorted)


def _pick(n, pref):
    while n % pref:
        pref //= 2
    return pref


def _forward(x_prompt, x_sample, cache_k, cache_v, state_rec, norm_mix_w, w_in, q_norm_w, k_norm_w,
             attn_sinks, rec_lb_logits, rec_norm_w, w_attn_branch, w_rec_branch, w_out, norm_ffn_w,
             router_w, router_b, w_gate_up, b_gate_up, w_down, b_down, moe_bm=1024, moe_tf=256):
    bp, sp, d = x_prompt.shape
    bs, ss, _ = x_sample.shape
    assert ss == CHUNK and sp % CHUNK == 0 and sp >= WINDOW
    tp, ts = bp * sp, bs * ss
    t = tp + ts
    xp2 = x_prompt.reshape(tp, d)
    xs2 = x_sample.reshape(ts, d)

    w = w_in[0]
    o_q, o_k, o_v, o_r = 0, Q_W, Q_W + KV_W, Q_W + 2 * KV_W
    o_g = o_r + 4 * REC_W
    w_perm = jnp.concatenate([w[:, o_g:], w[:, o_q:o_k], w[:, o_r:o_g], w[:, o_k:o_r]], axis=1).astype(BF16)
    c_aq = 2 * d
    c_rec = c_aq + Q_W
    c_k = c_rec + 4 * REC_W
    c_v = c_k + KV_W

    proj = _in_proj(xp2, xs2, norm_mix_w, w_perm, _pick(ts, 1024), 512)

    pos = jnp.concatenate([jnp.tile(jnp.arange(sp, dtype=I32), bp),
                           jnp.tile(PAST_LEN + jnp.arange(ss, dtype=I32), bs)]).astype(F32)
    half = HEAD_DIM // 2
    inv_freq = ROPE_THETA ** (-jnp.arange(half, dtype=F32) / half)
    ang = pos[:, None] * inv_freq[None, :]
    cos = jnp.tile(jnp.cos(ang), (1, LANES // half))
    sgn = jnp.tile(jnp.concatenate([-jnp.ones((half,), F32), jnp.ones((half,), F32)]), LANES // HEAD_DIM)
    sin = jnp.tile(jnp.sin(ang), (1, LANES // half)) * sgn[None, :]
    qw = jnp.tile(q_norm_w[0], LANES // HEAD_DIM)[None, :]
    kw = jnp.tile(k_norm_w[0], LANES // HEAD_DIM)[None, :]
    qr, kr = _qk_prep(proj, cos, sin, qw, kw, c_aq // Q_W, c_k // KV_W, _pick(ts, 512))

    sinks = attn_sinks[0]
    npc = sp // CHUNK
    vcol = c_v // KV_W

    assert npc % 2 == 0
    npp = npc // 2

    def prow(back):
        return lambda b, c: b * npp + jnp.maximum(c - back, 0)

    oa_p = _attention(sinks, qr, [(kr, 0)] * 2, [(proj, vcol)] * 2, (bp, npp),
                      prow(0), prow(0), tp, [prow(1), prow(0)], 2 * CHUNK, 2, True)
    ck = cache_k[0].reshape(bs * WINDOW, KV_W)
    cv = cache_v[0].reshape(bs * WINDOW, KV_W)
    srow = tp // CHUNK
    oa_s = _attention(sinks, qr, [(ck, 0), (ck, 0), (kr, 0)], [(cv, 0), (cv, 0), (proj, vcol)], (bs,),
                      lambda b: srow + b, lambda b: b, ts,
                      [lambda b: 2 * b, lambda b: 2 * b + 1, lambda b: srow + b], CHUNK, 1, False)

    lb = jax.nn.softmax(rec_lb_logits.astype(F32), axis=0)[0][None, :]
    rec_cols = tuple(c_rec + i * REC_W for i in range(4))
    nw_rec = rec_norm_w[0][None, :]
    tb = _pick(sp, 512)
    or_p, sfin_p = _hgrn2(proj, lb, nw_rec, jnp.zeros((bp, N_REC_HEADS, REC_D, REC_D), F32),
                          0, bp, sp, tb, rec_cols, tp)
    or_s, sfin_s = _hgrn2(proj, lb, nw_rec, state_rec[0], tp // CHUNK, bs, ss, CHUNK, rec_cols, ts)

    rw = jnp.pad(router_w[0], ((0, 0), (0, LANES - N_EXPERTS)))
    rwh = rw.astype(BF16)
    rwl = (rw - rwh.astype(F32)).astype(BF16)
    rb = jnp.concatenate([router_b[0].astype(F32), jnp.full((LANES - N_EXPERTS,), -jnp.inf, F32)])[None, :]
    tm = _pick(ts, 256)
    x1, h2, gt, te, lr, tcnt, tpre = _merge(xp2, xs2, oa_p, oa_s, or_p, or_s, proj, 0, 1,
                                            w_attn_branch[0].astype(BF16),
                                            w_rec_branch[0].astype(BF16), w_out[0].astype(BF16),
                                            norm_ffn_w, rwh, rwl, rb, tm)

    tcnt = tcnt[:, 0, :N_EXPERTS]
    counts = jnp.sum(tcnt, axis=0)
    padded = (counts + moe_bm - 1) // moe_bm * moe_bm
    pad_end = jnp.cumsum(padded)
    pad_start = pad_end - padded
    nb = (t * TOP_K) // moe_bm + N_EXPERTS
    block_start = jnp.arange(nb, dtype=I32) * moe_bm
    block_e = jnp.minimum(jnp.sum((pad_end[None, :] <= block_start[:, None]).astype(I32), axis=1), N_EXPERTS - 1)
    n_active = (pad_end[-1:] // moe_bm).astype(I32)
    n_valid = jnp.clip((pad_start + counts)[block_e] - block_start, 0, moe_bm).astype(I32)
    run_cnt = tcnt.reshape(-1)
    run_glb = (pad_start[None, :] + tpre[:, 0, :N_EXPERTS]).reshape(-1)

    xs = _dispatch(run_cnt, run_glb, h2, te, lr, nb * moe_bm, tm)
    o_sorted = _moe_blocks(block_e, n_active, n_valid, xs, w_gate_up[0], b_gate_up[0], w_down[0], b_down[0],
                           moe_bm, moe_tf)
    y_p, y_s = _combine(run_cnt, run_glb, x1, gt, te, lr, o_sorted, tm, tp)
    y_p = y_p.reshape(bp, sp, d)
    y_s = y_s.reshape(bs, ss, d)
    v_new = proj[:, c_v:c_v + KV_W]
    kp = kr[:tp].reshape(bp, sp, N_KV_HEADS, HEAD_DIM)[:, -WINDOW:]
    vp = v_new[:tp].reshape(bp, sp, N_KV_HEADS, HEAD_DIM)[:, -WINDOW:]
    ks_new = kr[tp:].reshape(bs, ss, N_KV_HEADS, HEAD_DIM)
    vs_new = v_new[tp:].reshape(bs, ss, N_KV_HEADS, HEAD_DIM)
    ks = jnp.concatenate([cache_k[0], ks_new], axis=1)[:, -WINDOW:]
    vs = jnp.concatenate([cache_v[0], vs_new], axis=1)[:, -WINDOW:]
    return (y_p, y_s, kp[None], vp[None], sfin_p[None], ks[None], vs[None], sfin_s[None])


def kernel(x_prompt, x_sample, cache_k, cache_v, state_rec, norm_mix_w, w_in, q_norm_w, k_norm_w, attn_sinks, rec_lb_logits, rec_norm_w, w_attn_branch, w_rec_branch, w_out, norm_ffn_w, router_w, router_b, w_gate_up, b_gate_up, w_down, b_down):
    return _forward(x_prompt, x_sample, cache_k, cache_v, state_rec, norm_mix_w, w_in, q_norm_w, k_norm_w,
                    attn_sinks, rec_lb_logits, rec_norm_w, w_attn_branch, w_rec_branch, w_out, norm_ffn_w,
                    router_w, router_b, w_gate_up, b_gate_up, w_down, b_down)
```

```python
import functools

import jax
import jax.numpy as jnp
from jax import lax
from jax.experimental import pallas as pl
from jax.experimental.pallas import tpu as pltpu

F32 = jnp.float32
BF16 = jnp.bfloat16
I32 = jnp.int32

CHUNK = 64
N_HEADS = 16
N_KV_HEADS = 4
HEAD_DIM = 64
WINDOW = 128
ROPE_THETA = 10000.0
N_REC_HEADS = 8
REC_D = 128
N_EXPERTS = 32
TOP_K = 4
SWIGLU_LIMIT = 7.0
SWIGLU_ALPHA = 1.702
EPS = 1e-6
PAST_LEN = 2048
LANES = 128
SUB = 16

VMEM_LIMIT = 56 * 1024 * 1024

Q_W = N_HEADS * HEAD_DIM
KV_W = N_KV_HEADS * HEAD_DIM
REC_W = N_REC_HEADS * REC_D


def _cparams(sem):
    return pltpu.CompilerParams(dimension_semantics=sem, vmem_limit_bytes=VMEM_LIMIT)


def _sigmoid(x):
    return 1.0 / (1.0 + jnp.exp(-x))


def _split2(x):
    hi = x.astype(BF16)
    lo = (x - hi.astype(F32)).astype(BF16)
    return hi, lo


def _dot(a, b):
    return jnp.dot(a, b, preferred_element_type=F32)


def _dot_nt(a, b):
    return lax.dot_general(a, b, (((1,), (1,)), ((), ())), preferred_element_type=F32)


def _two_group_specs(shape, n_first):
    first = pl.BlockSpec(shape, lambda i, *_: (jnp.minimum(i, n_first - 1), 0))
    second = pl.BlockSpec(shape, lambda i, *_: (jnp.maximum(i - n_first, 0), 0))
    return first, second


def _in_proj_kernel(xa_ref, xb_ref, nw_ref, w_ref, o_ref, h_scr, *, n_first):
    def norm(x_ref):
        x = x_ref[...]
        ms = jnp.mean(x * x, axis=-1, keepdims=True)
        h_scr[...] = (x * lax.rsqrt(ms + EPS) * nw_ref[...]).astype(BF16)

    first_col = pl.program_id(1) == 0
    in_first = pl.program_id(0) < n_first
    pl.when(jnp.logical_and(first_col, in_first))(lambda: norm(xa_ref))
    pl.when(jnp.logical_and(first_col, jnp.logical_not(in_first)))(lambda: norm(xb_ref))
    o_ref[...] = _dot(h_scr[...], w_ref[...])


def _in_proj(xa, xb, norm_w, w, tm, tn):
    d = xa.shape[1]
    t = xa.shape[0] + xb.shape[0]
    n = w.shape[1]
    n_first = xa.shape[0] // tm
    return pl.pallas_call(
        functools.partial(_in_proj_kernel, n_first=n_first),
        grid=(t // tm, n // tn),
        in_specs=[*_two_group_specs((tm, d), n_first),
                  pl.BlockSpec((1, d), lambda i, j: (0, 0)),
                  pl.BlockSpec((d, tn), lambda i, j: (0, j))],
        out_specs=pl.BlockSpec((tm, tn), lambda i, j: (i, j)),
        out_shape=jax.ShapeDtypeStruct((t, n), F32),
        scratch_shapes=[pltpu.VMEM((tm, d), BF16)],
        compiler_params=_cparams(("parallel", "arbitrary")),
        name="in_proj",
    )(xa, xb, norm_w, w)


def _norm_rope(x, nw, cos, sin, bd):
    hi, lo = _split2(x * x)
    ss = _dot(hi, bd) + _dot(lo, bd)
    y = x * lax.rsqrt(ss * (1.0 / HEAD_DIM) + EPS) * nw
    lane = lax.broadcasted_iota(I32, y.shape, 1)
    first_half = (lane % HEAD_DIM) < (HEAD_DIM // 2)
    rot = jnp.where(first_half, pltpu.roll(y, LANES - HEAD_DIM // 2, 1), pltpu.roll(y, HEAD_DIM // 2, 1))
    return y * cos + rot * sin


def _qk_prep_kernel(q_ref, k_ref, cos_ref, sin_ref, qw_ref, kw_ref, qo_ref, ko_ref):
    r = lax.broadcasted_iota(I32, (LANES, LANES), 0) // HEAD_DIM
    c = lax.broadcasted_iota(I32, (LANES, LANES), 1) // HEAD_DIM
    bd = jnp.where(r == c, 1.0, 0.0).astype(BF16)
    cos = cos_ref[...]
    sin = sin_ref[...]
    scale = HEAD_DIM ** -0.5
    for g in range(Q_W // LANES):
        sl = slice(g * LANES, (g + 1) * LANES)
        qo_ref[:, sl] = (_norm_rope(q_ref[:, sl], qw_ref[...], cos, sin, bd) * scale).astype(BF16)
    for g in range(KV_W // LANES):
        sl = slice(g * LANES, (g + 1) * LANES)
        ko_ref[:, sl] = _norm_rope(k_ref[:, sl], kw_ref[...], cos, sin, bd)


def _qk_prep(proj, cos, sin, qw, kw, q_blk, k_blk, tr):
    t = proj.shape[0]
    return pl.pallas_call(
        _qk_prep_kernel,
        grid=(t // tr,),
        in_specs=[pl.BlockSpec((tr, Q_W), lambda i: (i, q_blk)),
                  pl.BlockSpec((tr, KV_W), lambda i: (i, k_blk)),
                  pl.BlockSpec((tr, LANES), lambda i: (i, 0)),
                  pl.BlockSpec((tr, LANES), lambda i: (i, 0)),
                  pl.BlockSpec((1, LANES), lambda i: (0, 0)),
                  pl.BlockSpec((1, LANES), lambda i: (0, 0))],
        out_specs=[pl.BlockSpec((tr, Q_W), lambda i: (i, 0)),
                   pl.BlockSpec((tr, KV_W), lambda i: (i, 0))],
        out_shape=[jax.ShapeDtypeStruct((t, Q_W), BF16), jax.ShapeDtypeStruct((t, KV_W), F32)],
        compiler_params=_cparams(("parallel",)),
        name="qk_prep",
    )(proj, proj, cos, sin, qw, kw)


def _attn_kernel(sink_ref, q_ref, *refs, masked, n_kv, n_sub):
    k_all = jnp.concatenate([r[...] for r in refs[:n_kv]], axis=0)
    v_all = jnp.concatenate([r[...] for r in refs[n_kv:2 * n_kv]], axis=0)
    o_ref = refs[2 * n_kv]
    n_keys = 3 * CHUNK
    lane = lax.broadcasted_iota(I32, (n_keys, LANES), 1)
    low = lane < HEAD_DIM
    row = lax.broadcasted_iota(I32, (2 * CHUNK, 1), 0)
    for u in range(n_sub):
        rows = slice(u * CHUNK, (u + 1) * CHUNK)
        k_u = k_all[u * CHUNK:u * CHUNK + n_keys]
        v_u = v_all[u * CHUNK:u * CHUNK + n_keys]
        if masked:
            first = n_sub * pl.program_id(1) + u
            key_chunk = lax.broadcasted_iota(I32, (2 * CHUNK, n_keys), 1) // CHUNK
            valid = (key_chunk + first) >= 2
        for j in range(N_KV_HEADS):
            pair = slice((j // 2) * LANES, (j // 2 + 1) * LANES)
            kp = k_u[:, pair]
            vp = v_u[:, pair]
            kr = pltpu.roll(kp, HEAD_DIM, 1)
            vr = pltpu.roll(vp, HEAD_DIM, 1)
            if j % 2 == 0:
                k_low, k_high, v_low, v_high = kp, kr, vp, vr
            else:
                k_low, k_high, v_low, v_high = kr, kp, vr, vp
            q2 = jnp.concatenate([q_ref[rows, (2 * j) * LANES:(2 * j + 1) * LANES],
                                  q_ref[rows, (2 * j + 1) * LANES:(2 * j + 2) * LANES]], axis=0)
            o = jnp.zeros((2 * CHUNK, LANES), F32)
            for half, (kk, vv) in enumerate(((k_low, v_low), (k_high, v_high))):
                keep = low if half == 0 else jnp.logical_not(low)
                kz = jnp.where(keep, kk, 0.0).astype(BF16)
                vz = jnp.where(keep, vv, 0.0).astype(BF16)
                s = _dot_nt(q2, kz)
                if masked:
                    s = jnp.where(valid, s, -jnp.inf)
                sink = jnp.where(row < CHUNK, sink_ref[4 * j + half], sink_ref[4 * j + 2 + half])
                m = jnp.maximum(jnp.max(s, axis=-1, keepdims=True), sink)
                p = jnp.exp(s - m)
                p = p / (jnp.sum(p, axis=-1, keepdims=True) + jnp.exp(sink - m))
                o = o + _dot(p.astype(BF16), vz)
            o_ref[rows, (2 * j) * LANES:(2 * j + 1) * LANES] = o[:CHUNK].astype(BF16)
            o_ref[rows, (2 * j + 1) * LANES:(2 * j + 2) * LANES] = o[CHUNK:].astype(BF16)


def _attention(sinks, qr, k_srcs, v_srcs, grid, q_map, o_map, n_rows, kv_maps, kv_rows, n_sub, masked):
    assert kv_rows * len(k_srcs) == (n_sub + 2) * CHUNK

    def spec(col, fn):
        return pl.BlockSpec((kv_rows, KV_W), lambda *g: (fn(*g), col))

    in_specs = [pl.BlockSpec(memory_space=pltpu.SMEM),
                pl.BlockSpec((n_sub * CHUNK, Q_W), lambda *g: (q_map(*g), 0))]
    in_specs += [spec(col, fn) for (_, col), fn in zip(k_srcs, kv_maps)]
    in_specs += [spec(col, fn) for (_, col), fn in zip(v_srcs, kv_maps)]
    return pl.pallas_call(
        functools.partial(_attn_kernel, masked=masked, n_kv=len(k_srcs), n_sub=n_sub),
        grid=grid,
        in_specs=in_specs,
        out_specs=pl.BlockSpec((n_sub * CHUNK, Q_W), lambda *g: (o_map(*g), 0)),
        out_shape=jax.ShapeDtypeStruct((n_rows, Q_W), BF16),
        compiler_params=_cparams(("parallel",) * len(grid)),
        name="attn_prompt" if masked else "attn_sample",
    )(sinks, qr, *[a for a, _ in k_srcs], *[a for a, _ in v_srcs])


HEADS_PER_STEP = 8


def _hgrn2_chunk(r0, sl, hh, rq_ref, rf_ref, rv_ref, rg_ref, lb_ref, nw_ref, o_ref,
                 st_scr, q_scr, k_scr, v_scr, c_scr, tri, ones, rsub):
    n_sub = CHUNK // SUB
    rq = rq_ref[pl.ds(r0, CHUNK), sl]
    q = rq * _sigmoid(rq)
    k = (1.0 - lb_ref[:, sl]) * _sigmoid(-rf_ref[pl.ds(r0, CHUNK), sl])
    g = jnp.log1p(-k)
    v = rv_ref[pl.ds(r0, CHUNK), sl]
    g1 = g.astype(BF16)
    e1 = g - g1.astype(F32)
    g2 = e1.astype(BF16)
    g3 = (e1 - g2.astype(F32)).astype(BF16)
    cum = _dot(tri, g1) + _dot(tri, g2) + _dot(tri, g3)
    q_scr[hh] = q
    k_scr[hh] = k
    v_scr[hh] = v
    c_scr[hh] = cum
    st = st_scr[hh]
    inter = _dot_nt((q * jnp.exp(cum)).astype(BF16), st.astype(BF16))
    outs = []
    for i in range(n_sub):
        lo = i * SUB
        q_i = q_scr[hh, lo:lo + SUB, :]
        c_i = c_scr[hh, lo:lo + SUB, :]
        acc = inter[lo:lo + SUB, :]
        if i > 0:
            ref = c_scr[hh, lo - 1:lo, :]
            qd = (q_i * jnp.exp(c_i - ref)).astype(BF16)
            kd = (k_scr[hh, 0:lo, :] * jnp.exp(ref - c_scr[hh, 0:lo, :])).astype(BF16)
            s = _dot_nt(qd, kd)
            acc = acc + _dot(s.astype(BF16), v_scr[hh, 0:lo, :].astype(BF16))
        parts = []
        for s_ in range(SUB):
            arg = jnp.where(rsub >= s_, c_i - c_scr[hh, lo + s_:lo + s_ + 1, :], -jnp.inf)
            parts.append(q_i * k_scr[hh, lo + s_:lo + s_ + 1, :] * jnp.exp(arg))
        dsum = _dot(jnp.concatenate(parts, axis=0).astype(BF16), ones)
        for s_ in range(SUB):
            acc = acc + dsum[s_ * SUB:(s_ + 1) * SUB, :] * v_scr[hh, lo + s_:lo + s_ + 1, :]
        outs.append(acc)
    o = jnp.concatenate(outs, axis=0)
    last = c_scr[hh, CHUNK - 1:CHUNK, :]
    kdec = (k * jnp.exp(last - cum)).astype(BF16)
    st_scr[hh] = st * jnp.exp(last) + _dot(v.T.astype(BF16), kdec)
    ms = jnp.mean(o * o, axis=-1, keepdims=True)
    rg = rg_ref[pl.ds(r0, CHUNK), sl]
    o_ref[pl.ds(r0, CHUNK), sl] = (o * lax.rsqrt(ms + EPS) * nw_ref[...] * (rg * _sigmoid(rg))).astype(BF16)


def _hgrn2_kernel(rq_ref, rf_ref, rv_ref, rg_ref, lb_ref, nw_ref, s0_ref, o_ref, sfin_ref,
                  st_scr, q_scr, k_scr, v_scr, c_scr, *, n_chunks):
    t = pl.program_id(2)

    @pl.when(t == 0)
    def _():
        for hh in range(HEADS_PER_STEP):
            st_scr[hh] = s0_ref[0, hh].T

    r64 = lax.broadcasted_iota(I32, (CHUNK, CHUNK), 0)
    c64 = lax.broadcasted_iota(I32, (CHUNK, CHUNK), 1)
    tri = jnp.where(r64 >= c64, 1.0, 0.0).astype(BF16)
    ones = jnp.ones((LANES, LANES), BF16)
    rsub = lax.broadcasted_iota(I32, (SUB, LANES), 0)

    def chunk(ci, carry):
        r0 = pl.multiple_of(ci * CHUNK, CHUNK)
        for hh in range(HEADS_PER_STEP):
            sl = slice(hh * REC_D, (hh + 1) * REC_D)
            _hgrn2_chunk(r0, sl, hh, rq_ref, rf_ref, rv_ref, rg_ref, lb_ref, nw_ref, o_ref,
                         st_scr, q_scr, k_scr, v_scr, c_scr, tri, ones, rsub)
        return carry

    lax.fori_loop(0, n_chunks, chunk, 0)

    @pl.when(t == pl.num_programs(2) - 1)
    def _():
        for hh in range(HEADS_PER_STEP):
            sfin_ref[0, hh] = st_scr[hh].T


def _hgrn2(proj, lb, norm_w, s0, row_blk0, n_seq, seq_len, tb, col_blks, out_rows):
    nt = seq_len // tb
    hb = HEADS_PER_STEP
    wb = hb * REC_D
    cq, cf, cv, cg = col_blks

    def in_spec(c0):
        return pl.BlockSpec((tb, wb), lambda b, h, t: (row_blk0 + b * nt + t, c0 // wb + h))

    return pl.pallas_call(
        functools.partial(_hgrn2_kernel, n_chunks=tb // CHUNK),
        grid=(n_seq, N_REC_HEADS // hb, nt),
        in_specs=[in_spec(cq), in_spec(cf), in_spec(cv), in_spec(cg),
                  pl.BlockSpec((1, wb), lambda b, h, t: (0, h)),
                  pl.BlockSpec((1, REC_D), lambda b, h, t: (0, 0)),
                  pl.BlockSpec((1, hb, REC_D, REC_D), lambda b, h, t: (b, h, 0, 0))],
        out_specs=[pl.BlockSpec((tb, wb), lambda b, h, t: (b * nt + t, h)),
                   pl.BlockSpec((1, hb, REC_D, REC_D), lambda b, h, t: (b, h, 0, 0))],
        out_shape=[jax.ShapeDtypeStruct((out_rows, REC_W), BF16),
                   jax.ShapeDtypeStruct((n_seq, N_REC_HEADS, REC_D, REC_D), F32)],
        scratch_shapes=[pltpu.VMEM((hb, REC_D, REC_D), F32)] + [pltpu.VMEM((hb, CHUNK, REC_D), F32)] * 4,
        compiler_params=_cparams(("parallel", "parallel", "arbitrary")),
        name="hgrn2",
    )(proj, proj, proj, proj, lb, norm_w, s0)


def _pack_bf16_pairs(h):
    d = h.shape[1]
    bits = lax.bitcast_convert_type(h, jnp.uint32)
    return (bits[:, d // 2:] & jnp.uint32(0xFFFF0000)) | (bits[:, :d // 2] >> 16)


def _merge_kernel(xa_ref, xb_ref, oaa_ref, oab_ref, ora_ref, orb_ref, ga_ref, gr_ref, wa_ref, wr_ref, wo_ref,
                  nf_ref, rwh_ref, rwl_ref, rb_ref,
                  x1_ref, h2_ref, gt_ref, te_ref, lr_ref, tcnt_ref, tpre_ref, carry_scr, *, n_first):
    i = pl.program_id(0)

    @pl.when(i == 0)
    def _():
        carry_scr[...] = jnp.zeros_like(carry_scr)

    in_first = i < n_first
    a = _dot(jnp.where(in_first, oaa_ref[...], oab_ref[...]), wa_ref[...])
    r = _dot(jnp.where(in_first, ora_ref[...], orb_ref[...]), wr_ref[...])
    merged = _sigmoid(ga_ref[...]) * a + _sigmoid(gr_ref[...]) * r
    x1 = jnp.where(in_first, xa_ref[...], xb_ref[...]) + _dot(merged.astype(BF16), wo_ref[...])
    x1_ref[...] = x1
    ms = jnp.mean(x1 * x1, axis=-1, keepdims=True)
    h2 = x1 * lax.rsqrt(ms + EPS) * nf_ref[...]
    tm = h2.shape[0]
    hh, hl = _split2(h2)
    logits = _dot(hh, rwh_ref[...]) + _dot(hl, rwh_ref[...]) + _dot(hh, rwl_ref[...]) + rb_ref[...]
    lane = lax.broadcasted_iota(I32, (tm, LANES), 1)
    lane_f = lane.astype(F32)
    work = logits
    h2_ref[...] = hh
    te = jnp.zeros((tm, LANES), I32)
    tv = jnp.zeros((tm, LANES), F32)
    onehot = jnp.zeros((tm, LANES), F32)
    sels = []
    for k in range(TOP_K):
        m = jnp.max(work, axis=-1, keepdims=True)
        idx = jnp.min(jnp.where(work == m, lane_f, float(LANES)), axis=-1, keepdims=True).astype(I32)
        sel = lane == idx
        work = jnp.where(sel, -jnp.inf, work)
        onehot = onehot + jnp.where(sel, 1.0, 0.0)
        te = jnp.where(lane == k, idx, te)
        tv = jnp.where(lane == k, m, tv)
        sels.append(sel)
    ex = jnp.where(lane < TOP_K, jnp.exp(tv - tv[:, 0:1]), 0.0)
    gt_ref[...] = ex / jnp.sum(ex, axis=-1, keepdims=True)
    te_ref[...] = te
    rr = lax.broadcasted_iota(I32, (tm, tm), 0)
    cc = lax.broadcasted_iota(I32, (tm, tm), 1)
    strict = jnp.where(rr > cc, 1.0, 0.0).astype(BF16)
    lrank = _dot(strict, onehot.astype(BF16))
    lr = jnp.zeros((tm, LANES), F32)
    for k in range(TOP_K):
        lr = jnp.where(lane == k, jnp.sum(jnp.where(sels[k], lrank, 0.0), axis=-1, keepdims=True), lr)
    lr_ref[...] = lr.astype(I32)
    tcnt = jnp.sum(onehot, axis=0, keepdims=True)
    tcnt_ref[0] = tcnt.astype(I32)
    tpre_ref[0] = carry_scr[...].astype(I32)
    carry_scr[...] = carry_scr[...] + tcnt


def _merge(xa, xb, oaa, oab, ora, orb, proj, ga_blk, gr_blk, wa, wr, wo, nf, rwh, rwl, rb, tm):
    d = xa.shape[1]
    t = xa.shape[0] + xb.shape[0]
    nt = t // tm
    n_first = xa.shape[0] // tm

    def const(shape):
        return pl.BlockSpec(shape, lambda i: (0,) * len(shape), pipeline_mode=pl.Buffered(1))

    tile_vec = pl.BlockSpec((1, 1, LANES), lambda i: (i, 0, 0))
    return pl.pallas_call(
        functools.partial(_merge_kernel, n_first=n_first),
        grid=(nt,),
        in_specs=[*_two_group_specs((tm, d), n_first),
                  *_two_group_specs((tm, Q_W), n_first),
                  *_two_group_specs((tm, REC_W), n_first),
                  pl.BlockSpec((tm, d), lambda i: (i, ga_blk)),
                  pl.BlockSpec((tm, d), lambda i: (i, gr_blk)),
                  const((Q_W, d)), const((REC_W, d)), const((d, d)), const((1, d)),
                  const((d, LANES)), const((d, LANES)), const((1, LANES))],
        out_specs=[pl.BlockSpec((tm, d), lambda i: (i, 0)),
                   pl.BlockSpec((tm, d), lambda i: (i, 0)),
                   pl.BlockSpec((tm, LANES), lambda i: (i, 0)),
                   pl.BlockSpec((tm, LANES), lambda i: (i, 0)),
                   pl.BlockSpec((tm, LANES), lambda i: (i, 0)),
                   tile_vec, tile_vec],
        out_shape=[jax.ShapeDtypeStruct((t, d), F32), jax.ShapeDtypeStruct((t, d), BF16),
                   jax.ShapeDtypeStruct((t, LANES), F32), jax.ShapeDtypeStruct((t, LANES), I32),
                   jax.ShapeDtypeStruct((t, LANES), I32),
                   jax.ShapeDtypeStruct((nt, 1, LANES), I32), jax.ShapeDtypeStruct((nt, 1, LANES), I32)],
        scratch_shapes=[pltpu.VMEM((1, LANES), F32)],
        compiler_params=_cparams(("arbitrary",)),
        name="merge_router",
    )(xa, xb, oaa, oab, ora, orb, proj, proj, wa, wr, wo, nf, rwh, rwl, rb)


SUBLANES = 8
RUN_BITS = 9


def _local_rows(tm):
    return TOP_K * tm + SUBLANES * N_EXPERTS


def _run_layout(cnt_ref, glb_ref, tile, loc_scr):
    def body(e, nxt):
        j = tile * N_EXPERTS + e
        loc = nxt + ((glb_ref[j] - nxt) & (SUBLANES - 1))
        loc_scr[e] = loc
        return loc + cnt_ref[j]

    lax.fori_loop(0, N_EXPERTS, body, 0)


def _run_copies(cnt_ref, glb_ref, loc_scr, tile, make_copy, action, n_bits=RUN_BITS):
    def per_expert(e, c):
        j = tile * N_EXPERTS + e
        n = cnt_ref[j]
        glb = glb_ref[j]
        loc = loc_scr[e]
        head = jnp.minimum(n, (-glb) & (SUBLANES - 1))
        mid = ((n - head) // SUBLANES) * SUBLANES

        def single(i, c2):
            action(make_copy(loc + i, glb + i, 1))
            return c2

        lax.fori_loop(0, head, single, 0)
        for bit in range(n_bits - 1, 2, -1):
            size = 1 << bit

            @pl.when((mid & size) != 0)
            def _():
                done = head + ((mid >> (bit + 1)) << (bit + 1))
                action(make_copy(pl.multiple_of(loc + done, SUBLANES), pl.multiple_of(glb + done, SUBLANES), size))

        lax.fori_loop(head + mid, n, single, 0)
        return c

    lax.fori_loop(0, N_EXPERTS, per_expert, 0)


def _positions(te, lr, loc_scr, n_local):
    tm = te.shape[0]
    lane = lax.broadcasted_iota(I32, (1, LANES), 1)
    lvec = jnp.zeros((1, LANES), I32)
    for e in range(N_EXPERTS):
        lvec = jnp.where(lane == e, loc_scr[e], lvec)
    lvec = lvec.astype(F32)
    lane_t = lax.broadcasted_iota(I32, (tm, LANES), 1)
    pos = []
    for k in range(TOP_K):
        start = jnp.sum(jnp.where(lane_t == te[:, k:k + 1], lvec, 0.0), axis=-1, keepdims=True)
        pos.append(start.astype(I32) + lr[:, k:k + 1])
    return pos, lax.broadcasted_iota(I32, (tm, n_local), 1)


def _dispatch_kernel(cnt_ref, glb_ref, padn_ref, padlo_ref, h_ref, te_ref, lr_ref, xs_ref, srt_scr, loc_scr, sem, *,
                     pad_bits):
    tile = pl.program_id(0)
    _run_layout(cnt_ref, glb_ref, tile, loc_scr)

    @pl.when(tile == 0)
    def _():
        n_zero = 1 << (pad_bits - 1)
        srt_scr[0:n_zero, :] = jnp.zeros((n_zero, srt_scr.shape[1]), srt_scr.dtype)

        def zero_copy(loc, glb, size):
            return pltpu.make_async_copy(srt_scr.at[pl.ds(0, size)], xs_ref.at[pl.ds(glb, size)], sem)

        _run_copies(padn_ref, padlo_ref, loc_scr, 0, zero_copy, lambda cp: cp.start(), pad_bits)
        _run_copies(padn_ref, padlo_ref, loc_scr, 0, zero_copy, lambda cp: cp.wait(), pad_bits)

        def tail(action):
            def body(i, c):
                start = pl.multiple_of(padlo_ref[N_EXPERTS] + i * n_zero, SUBLANES)
                action(zero_copy(0, start, n_zero))
                return c
            lax.fori_loop(0, padn_ref[N_EXPERTS], body, 0)

        tail(lambda cp: cp.start())
        tail(lambda cp: cp.wait())

    pos, slot_iota = _positions(te_ref[...], lr_ref[...], loc_scr, srt_scr.shape[0])
    perm_t = jnp.zeros(slot_iota.shape, F32)
    for k in range(TOP_K):
        perm_t = perm_t + jnp.where(slot_iota == pos[k], 1.0, 0.0)
    rows = lax.dot_general(perm_t.astype(BF16), h_ref[...], (((0,), (0,)), ((), ())), preferred_element_type=F32)
    srt_scr[...] = _pack_bf16_pairs(rows)

    def make_copy(loc, glb, size):
        return pltpu.make_async_copy(srt_scr.at[pl.ds(loc, size)], xs_ref.at[pl.ds(glb, size)], sem)

    _run_copies(cnt_ref, glb_ref, loc_scr, tile, make_copy, lambda cp: cp.start())
    _run_copies(cnt_ref, glb_ref, loc_scr, tile, make_copy, lambda cp: cp.wait())


def _dispatch(cnt, glb, pad_n, pad_lo, h2, te, lr, n_rows, tm, bm):
    t, d = h2.shape
    assert tm < (1 << RUN_BITS)
    pad_bits = (bm - 1).bit_length()
    assert bm == 1 << pad_bits and bm // 2 <= _local_rows(tm)
    grid_spec = pltpu.PrefetchScalarGridSpec(
        num_scalar_prefetch=4,
        grid=(t // tm,),
        in_specs=[pl.BlockSpec((tm, d), lambda i, *_: (i, 0)),
                  pl.BlockSpec((tm, LANES), lambda i, *_: (i, 0)),
                  pl.BlockSpec((tm, LANES), lambda i, *_: (i, 0))],
        out_specs=pl.BlockSpec(memory_space=pl.ANY),
        scratch_shapes=[pltpu.VMEM((_local_rows(tm), d // 2), jnp.uint32), pltpu.SMEM((N_EXPERTS,), I32),
                        pltpu.SemaphoreType.DMA],
    )
    return pl.pallas_call(
        functools.partial(_dispatch_kernel, pad_bits=pad_bits),
        grid_spec=grid_spec,
        out_shape=jax.ShapeDtypeStruct((n_rows, d // 2), jnp.uint32),
        compiler_params=_cparams(("arbitrary",)),
        name="moe_dispatch",
    )(cnt, glb, pad_n, pad_lo, h2, te, lr)


def _moe_kernel(be_ref, na_ref, nv_ref, x_ref, wg_ref, wu_ref, wd_ref, bgu_ref, bd_ref, o_ref, xb_scr):
    b = pl.program_id(0)
    f = pl.program_id(1)
    nf = pl.num_programs(1)
    bm = o_ref.shape[0]
    half = x_ref.shape[1]
    bg = bgu_ref[pl.ds(f, 1), :]
    bu = bgu_ref[pl.ds(nf + f, 1), :]

    def step(rows):
        @pl.when(f == 0)
        def _():
            pk = x_ref[0:rows, :]
            xb_scr[0:rows, :half] = lax.bitcast_convert_type(pk << 16, F32).astype(BF16)
            xb_scr[0:rows, half:] = lax.bitcast_convert_type(pk & jnp.uint32(0xFFFF0000), F32).astype(BF16)
            o_ref[0:rows, :] = jnp.broadcast_to(bd_ref[...], (rows, o_ref.shape[1]))
            if rows < bm:
                o_ref[rows:, :] = jnp.zeros((bm - rows, o_ref.shape[1]), F32)

        xb = xb_scr[0:rows, :]
        g = _dot(xb, wg_ref[...].astype(BF16)) + bg
        u = _dot(xb, wu_ref[...].astype(BF16)) + bu
        g = jnp.minimum(g, SWIGLU_LIMIT)
        u = jnp.clip(u, -SWIGLU_LIMIT, SWIGLU_LIMIT)
        act = (u + 1.0) * (g * _sigmoid(SWIGLU_ALPHA * g))
        o_ref[0:rows, :] += _dot(act.astype(BF16), wd_ref[...].astype(BF16))

    active = b < na_ref[0]
    full = nv_ref[b] > bm // 2
    pl.when(jnp.logical_and(active, full))(lambda: step(bm))
    pl.when(jnp.logical_and(active, jnp.logical_not(full)))(lambda: step(bm // 2))

    @pl.when(jnp.logical_and(jnp.logical_not(active), f == 0))
    def _():
        o_ref[...] = jnp.zeros_like(o_ref)


def _moe_blocks(block_e, n_active, n_valid, xs, w_gate_up, b_gate_up, w_down, b_down, bm, tf):
    n_exp, d, two_f = w_gate_up.shape
    n_rows = xs.shape[0]
    d_ff = two_f // 2
    nb = n_rows // bm
    nf = d_ff // tf

    def blk(b, na):
        return jnp.minimum(b, na[0] - 1)

    def fi(b, f, na):
        return jnp.where(b < na[0], f, nf - 1)

    grid_spec = pltpu.PrefetchScalarGridSpec(
        num_scalar_prefetch=3,
        grid=(nb, nf),
        in_specs=[pl.BlockSpec((bm, d // 2), lambda b, f, be, na, nv: (blk(b, na), 0)),
                  pl.BlockSpec((None, d, tf), lambda b, f, be, na, nv: (be[blk(b, na)], 0, fi(b, f, na))),
                  pl.BlockSpec((None, d, tf), lambda b, f, be, na, nv: (be[blk(b, na)], 0, fi(b, f, na) + nf)),
                  pl.BlockSpec((None, tf, d), lambda b, f, be, na, nv: (be[blk(b, na)], fi(b, f, na), 0)),
                  pl.BlockSpec((None, 2 * nf, tf), lambda b, f, be, na, nv: (be[blk(b, na)], 0, 0)),
                  pl.BlockSpec((None, 1, d), lambda b, f, be, na, nv: (be[blk(b, na)], 0, 0))],
        out_specs=pl.BlockSpec((bm, d), lambda b, f, be, na, nv: (b, 0)),
        scratch_shapes=[pltpu.VMEM((bm, d), BF16)],
    )
    return pl.pallas_call(
        _moe_kernel,
        grid_spec=grid_spec,
        out_shape=jax.ShapeDtypeStruct((n_rows, d), F32),
        compiler_params=_cparams(("arbitrary", "arbitrary")),
        name="moe_experts",
    )(block_e, n_active, n_valid, xs, w_gate_up, w_gate_up, w_down,
      b_gate_up.reshape(n_exp, 2 * nf, tf), b_down.reshape(n_exp, 1, d))


def _combine_kernel(cnt_ref, glb_ref, x1_ref, gt_ref, te_ref, lr_ref, os_ref, ya_ref, yb_ref, buf, loc_scr, sem, *,
                    n_first):
    tile = pl.program_id(0)

    @pl.when(tile == 0)
    def _():
        buf[...] = jnp.zeros_like(buf)

    _run_layout(cnt_ref, glb_ref, tile, loc_scr)

    def make_copy(loc, glb, size):
        return pltpu.make_async_copy(os_ref.at[pl.ds(glb, size)], buf.at[pl.ds(loc, size)], sem)

    _run_copies(cnt_ref, glb_ref, loc_scr, tile, make_copy, lambda cp: cp.start())
    pos, slot_iota = _positions(te_ref[...], lr_ref[...], loc_scr, buf.shape[0])
    gt = gt_ref[...]
    wsel = jnp.zeros(slot_iota.shape, F32)
    for k in range(TOP_K):
        wsel = wsel + jnp.where(slot_iota == pos[k], gt[:, k:k + 1], 0.0)
    _run_copies(cnt_ref, glb_ref, loc_scr, tile, make_copy, lambda cp: cp.wait())
    wh, wl = _split2(wsel)
    oh, ol = _split2(buf[...])
    y = x1_ref[...] + (_dot(wh, oh) + _dot(wl, oh) + _dot(wh, ol))

    @pl.when(tile < n_first)
    def _():
        ya_ref[...] = y

    @pl.when(tile >= n_first)
    def _():
        yb_ref[...] = y


def _combine(cnt, glb, x1, gates, te, lr, o_sorted, tm, t_first):
    t, d = x1.shape
    n_first = t_first // tm
    grid_spec = pltpu.PrefetchScalarGridSpec(
        num_scalar_prefetch=2,
        grid=(t // tm,),
        in_specs=[pl.BlockSpec((tm, d), lambda i, *_: (i, 0)),
                  pl.BlockSpec((tm, LANES), lambda i, *_: (i, 0)),
                  pl.BlockSpec((tm, LANES), lambda i, *_: (i, 0)),
                  pl.BlockSpec((tm, LANES), lambda i, *_: (i, 0)),
                  pl.BlockSpec(memory_space=pl.ANY)],
        out_specs=list(_two_group_specs((tm, d), n_first)),
        scratch_shapes=[pltpu.VMEM((_local_rows(tm), d), F32), pltpu.SMEM((N_EXPERTS,), I32),
                        pltpu.SemaphoreType.DMA],
    )
    return pl.pallas_call(
        functools.partial(_combine_kernel, n_first=n_first),
        grid_spec=grid_spec,
        out_shape=[jax.ShapeDtypeStruct((t_first, d), F32), jax.ShapeDtypeStruct((t - t_first, d), F32)],
        compiler_params=_cparams(("arbitrary",)),
        name="moe_combine",
    )(cnt, glb, x1, gates, te, lr, o_sorted)


def _pick(n, pref):
    while n % pref:
        pref //= 2
    return pref


def _forward(x_prompt, x_sample, cache_k, cache_v, state_rec, norm_mix_w, w_in, q_norm_w, k_norm_w,
             attn_sinks, rec_lb_logits, rec_norm_w, w_attn_branch, w_rec_branch, w_out, norm_ffn_w,
             router_w, router_b, w_gate_up, b_gate_up, w_down, b_down, moe_bm=1024, moe_tf=256):
    bp, sp, d = x_prompt.shape
    bs, ss, _ = x_sample.shape
    assert ss == CHUNK and sp % CHUNK == 0 and sp >= WINDOW
    tp, ts = bp * sp, bs * ss
    t = tp + ts
    xp2 = x_prompt.reshape(tp, d)
    xs2 = x_sample.reshape(ts, d)

    w = w_in[0]
    o_q, o_k, o_v, o_r = 0, Q_W, Q_W + KV_W, Q_W + 2 * KV_W
    o_g = o_r + 4 * REC_W
    w_perm = jnp.concatenate([w[:, o_g:], w[:, o_q:o_k], w[:, o_r:o_g], w[:, o_k:o_r]], axis=1).astype(BF16)
    c_aq = 2 * d
    c_rec = c_aq + Q_W
    c_k = c_rec + 4 * REC_W
    c_v = c_k + KV_W

    proj = _in_proj(xp2, xs2, norm_mix_w, w_perm, _pick(ts, 512), w_perm.shape[1] // 4)

    pos = jnp.concatenate([jnp.tile(jnp.arange(sp, dtype=I32), bp),
                           jnp.tile(PAST_LEN + jnp.arange(ss, dtype=I32), bs)]).astype(F32)
    half = HEAD_DIM // 2
    inv_freq = ROPE_THETA ** (-jnp.arange(half, dtype=F32) / half)
    ang = pos[:, None] * inv_freq[None, :]
    cos = jnp.tile(jnp.cos(ang), (1, LANES // half))
    sgn = jnp.tile(jnp.concatenate([-jnp.ones((half,), F32), jnp.ones((half,), F32)]), LANES // HEAD_DIM)
    sin = jnp.tile(jnp.sin(ang), (1, LANES // half)) * sgn[None, :]
    qw = jnp.tile(q_norm_w[0], LANES // HEAD_DIM)[None, :]
    kw = jnp.tile(k_norm_w[0], LANES // HEAD_DIM)[None, :]
    qr, kr = _qk_prep(proj, cos, sin, qw, kw, c_aq // Q_W, c_k // KV_W, _pick(ts, 512))

    sinks = attn_sinks[0]
    npc = sp // CHUNK
    vcol = c_v // KV_W

    assert npc % 2 == 0
    npp = npc // 2

    def prow(back):
        return lambda b, c: b * npp + jnp.maximum(c - back, 0)

    oa_p = _attention(sinks, qr, [(kr, 0)] * 2, [(proj, vcol)] * 2, (bp, npp),
                      prow(0), prow(0), tp, [prow(1), prow(0)], 2 * CHUNK, 2, True)
    ck = cache_k[0].reshape(bs * WINDOW, KV_W)
    cv = cache_v[0].reshape(bs * WINDOW, KV_W)
    srow = tp // CHUNK
    oa_s = _attention(sinks, qr, [(ck, 0), (ck, 0), (kr, 0)], [(cv, 0), (cv, 0), (proj, vcol)], (bs,),
                      lambda b: srow + b, lambda b: b, ts,
                      [lambda b: 2 * b, lambda b: 2 * b + 1, lambda b: srow + b], CHUNK, 1, False)

    lb = jax.nn.softmax(rec_lb_logits.astype(F32), axis=0)[0][None, :]
    rec_cols = tuple(c_rec + i * REC_W for i in range(4))
    nw_rec = rec_norm_w[0][None, :]
    tb = _pick(sp, 512)
    or_p, sfin_p = _hgrn2(proj, lb, nw_rec, jnp.zeros((bp, N_REC_HEADS, REC_D, REC_D), F32),
                          0, bp, sp, tb, rec_cols, tp)
    or_s, sfin_s = _hgrn2(proj, lb, nw_rec, state_rec[0], tp // CHUNK, bs, ss, CHUNK, rec_cols, ts)

    rw = jnp.pad(router_w[0], ((0, 0), (0, LANES - N_EXPERTS)))
    rwh = rw.astype(BF16)
    rwl = (rw - rwh.astype(F32)).astype(BF16)
    rb = jnp.concatenate([router_b[0].astype(F32), jnp.full((LANES - N_EXPERTS,), -jnp.inf, F32)])[None, :]
    tm = _pick(ts, 256)
    x1, h2, gt, te, lr, tcnt, tpre = _merge(xp2, xs2, oa_p, oa_s, or_p, or_s, proj, 0, 1,
                                            w_attn_branch[0].astype(BF16),
                                            w_rec_branch[0].astype(BF16), w_out[0].astype(BF16),
                                            norm_ffn_w, rwh, rwl, rb, tm)

    tcnt = tcnt[:, 0, :N_EXPERTS]
    counts = jnp.sum(tcnt, axis=0)
    padded = (counts + moe_bm - 1) // moe_bm * moe_bm
    pad_end = jnp.cumsum(padded)
    pad_start = pad_end - padded
    nb = (t * TOP_K) // moe_bm + N_EXPERTS
    block_start = jnp.arange(nb, dtype=I32) * moe_bm
    block_e = jnp.minimum(jnp.sum((pad_end[None, :] <= block_start[:, None]).astype(I32), axis=1), N_EXPERTS - 1)
    n_active = (pad_end[-1:] // moe_bm).astype(I32)
    n_valid = jnp.clip((pad_start + counts)[block_e] - block_start, 0, moe_bm).astype(I32)
    run_cnt = tcnt.reshape(-1)
    run_glb = (pad_start[None, :] + tpre[:, 0, :N_EXPERTS]).reshape(-1)

    pad_n = jnp.concatenate([padded - counts, (nb * moe_bm - pad_end[-1:]) // (moe_bm // 2)]).astype(I32)
    pad_lo = jnp.concatenate([pad_start + counts, pad_end[-1:]]).astype(I32)
    xs = _dispatch(run_cnt, run_glb, pad_n, pad_lo, h2, te, lr, nb * moe_bm, tm, moe_bm)
    o_sorted = _moe_blocks(block_e, n_active, n_valid, xs, w_gate_up[0], b_gate_up[0], w_down[0], b_down[0],
                           moe_bm, moe_tf)
    y_p, y_s = _combine(run_cnt, run_glb, x1, gt, te, lr, o_sorted, tm, tp)
    y_p = y_p.reshape(bp, sp, d)
    y_s = y_s.reshape(bs, ss, d)
    v_new = proj[:, c_v:c_v + KV_W]
    kp = kr[:tp].reshape(bp, sp, N_KV_HEADS, HEAD_DIM)[:, -WINDOW:]
    vp = v_new[:tp].reshape(bp, sp, N_KV_HEADS, HEAD_DIM)[:, -WINDOW:]
    ks_new = kr[tp:].reshape(bs, ss, N_KV_HEADS, HEAD_DIM)
    vs_new = v_new[tp:].reshape(bs, ss, N_KV_HEADS, HEAD_DIM)
    ks = jnp.concatenate([cache_k[0], ks_new], axis=1)[:, -WINDOW:]
    vs = jnp.concatenate([cache_v[0], vs_new], axis=1)[:, -WINDOW:]
    return (y_p, y_s, kp[None], vp[None], sfin_p[None], ks[None], vs[None], sfin_s[None])


def kernel(x_prompt, x_sample, cache_k, cache_v, state_rec, norm_mix_w, w_in, q_norm_w, k_norm_w, attn_sinks, rec_lb_logits, rec_norm_w, w_attn_branch, w_rec_branch, w_out, norm_ffn_w, router_w, router_b, w_gate_up, b_gate_up, w_down, b_down):
    return _forward(x_prompt, x_sample, cache_k, cache_v, state_rec, norm_mix_w, w_in, q_norm_w, k_norm_w,
                    attn_sinks, rec_lb_logits, rec_norm_w, w_attn_branch, w_rec_branch, w_out, norm_ffn_w,
                    router_w, router_b, w_gate_up, b_gate_up, w_down, b_down)
```

```python
import functools

import jax
import jax.numpy as jnp
from jax import lax
from jax.experimental import pallas as pl
from jax.experimental.pallas import tpu as pltpu

F32 = jnp.float32
BF16 = jnp.bfloat16
I32 = jnp.int32

CHUNK = 64
N_HEADS = 16
N_KV_HEADS = 4
HEAD_DIM = 64
WINDOW = 128
ROPE_THETA = 10000.0
N_REC_HEADS = 8
REC_D = 128
N_EXPERTS = 32
TOP_K = 4
SWIGLU_LIMIT = 7.0
SWIGLU_ALPHA = 1.702
EPS = 1e-6
PAST_LEN = 2048
LANES = 128
SUB = 16

VMEM_LIMIT = 56 * 1024 * 1024

Q_W = N_HEADS * HEAD_DIM
KV_W = N_KV_HEADS * HEAD_DIM
REC_W = N_REC_HEADS * REC_D


def _cparams(sem, vmem_limit=VMEM_LIMIT):
    return pltpu.CompilerParams(dimension_semantics=sem, vmem_limit_bytes=vmem_limit)


def _sigmoid(x):
    return 1.0 / (1.0 + jnp.exp(-x))


def _split2(x):
    hi = x.astype(BF16)
    lo = (x - hi.astype(F32)).astype(BF16)
    return hi, lo


def _dot(a, b):
    return jnp.dot(a, b, preferred_element_type=F32)


def _dot_nt(a, b):
    return lax.dot_general(a, b, (((1,), (1,)), ((), ())), preferred_element_type=F32)


def _two_group_specs(shape, n_first):
    first = pl.BlockSpec(shape, lambda i, *_: (jnp.minimum(i, n_first - 1), 0))
    second = pl.BlockSpec(shape, lambda i, *_: (jnp.maximum(i - n_first, 0), 0))
    return first, second


def _in_proj_kernel(xa_ref, xb_ref, nw_ref, w_ref, o_ref, h_scr, *, n_first):
    def norm(x_ref):
        x = x_ref[...]
        ms = jnp.mean(x * x, axis=-1, keepdims=True)
        h_scr[...] = (x * lax.rsqrt(ms + EPS) * nw_ref[...]).astype(BF16)

    first_col = pl.program_id(1) == 0
    in_first = pl.program_id(0) < n_first
    pl.when(jnp.logical_and(first_col, in_first))(lambda: norm(xa_ref))
    pl.when(jnp.logical_and(first_col, jnp.logical_not(in_first)))(lambda: norm(xb_ref))
    o_ref[...] = _dot(h_scr[...], w_ref[...])


def _in_proj(xa, xb, norm_w, w, tm, tn):
    d = xa.shape[1]
    t = xa.shape[0] + xb.shape[0]
    n = w.shape[1]
    n_first = xa.shape[0] // tm
    return pl.pallas_call(
        functools.partial(_in_proj_kernel, n_first=n_first),
        grid=(t // tm, n // tn),
        in_specs=[*_two_group_specs((tm, d), n_first),
                  pl.BlockSpec((1, d), lambda i, j: (0, 0)),
                  pl.BlockSpec((d, tn), lambda i, j: (0, j))],
        out_specs=pl.BlockSpec((tm, tn), lambda i, j: (i, j)),
        out_shape=jax.ShapeDtypeStruct((t, n), F32),
        scratch_shapes=[pltpu.VMEM((tm, d), BF16)],
        compiler_params=_cparams(("parallel", "arbitrary")),
        name="in_proj",
    )(xa, xb, norm_w, w)


def _norm_rope(x, nw, cos, sin, bd):
    hi, lo = _split2(x * x)
    ss = _dot(hi, bd) + _dot(lo, bd)
    y = x * lax.rsqrt(ss * (1.0 / HEAD_DIM) + EPS) * nw
    lane = lax.broadcasted_iota(I32, y.shape, 1)
    first_half = (lane % HEAD_DIM) < (HEAD_DIM // 2)
    rot = jnp.where(first_half, pltpu.roll(y, LANES - HEAD_DIM // 2, 1), pltpu.roll(y, HEAD_DIM // 2, 1))
    return y * cos + rot * sin


def _qk_prep_kernel(q_ref, k_ref, cos_ref, sin_ref, qw_ref, kw_ref, qo_ref, ko_ref):
    r = lax.broadcasted_iota(I32, (LANES, LANES), 0) // HEAD_DIM
    c = lax.broadcasted_iota(I32, (LANES, LANES), 1) // HEAD_DIM
    bd = jnp.where(r == c, 1.0, 0.0).astype(BF16)
    cos = cos_ref[...]
    sin = sin_ref[...]
    scale = HEAD_DIM ** -0.5
    for g in range(Q_W // LANES):
        sl = slice(g * LANES, (g + 1) * LANES)
        qo_ref[:, sl] = (_norm_rope(q_ref[:, sl], qw_ref[...], cos, sin, bd) * scale).astype(BF16)
    for g in range(KV_W // LANES):
        sl = slice(g * LANES, (g + 1) * LANES)
        ko_ref[:, sl] = _norm_rope(k_ref[:, sl], kw_ref[...], cos, sin, bd)


def _qk_prep(proj, cos, sin, qw, kw, q_blk, k_blk, tr):
    t = proj.shape[0]
    return pl.pallas_call(
        _qk_prep_kernel,
        grid=(t // tr,),
        in_specs=[pl.BlockSpec((tr, Q_W), lambda i: (i, q_blk)),
                  pl.BlockSpec((tr, KV_W), lambda i: (i, k_blk)),
                  pl.BlockSpec((tr, LANES), lambda i: (i, 0)),
                  pl.BlockSpec((tr, LANES), lambda i: (i, 0)),
                  pl.BlockSpec((1, LANES), lambda i: (0, 0)),
                  pl.BlockSpec((1, LANES), lambda i: (0, 0))],
        out_specs=[pl.BlockSpec((tr, Q_W), lambda i: (i, 0)),
                   pl.BlockSpec((tr, KV_W), lambda i: (i, 0))],
        out_shape=[jax.ShapeDtypeStruct((t, Q_W), BF16), jax.ShapeDtypeStruct((t, KV_W), F32)],
        compiler_params=_cparams(("parallel",)),
        name="qk_prep",
    )(proj, proj, cos, sin, qw, kw)


def _attn_kernel(sink_ref, q_ref, *refs, masked, n_kv, n_sub):
    k_all = jnp.concatenate([r[...] for r in refs[:n_kv]], axis=0)
    v_all = jnp.concatenate([r[...] for r in refs[n_kv:2 * n_kv]], axis=0)
    o_ref = refs[2 * n_kv]
    n_keys = 3 * CHUNK
    lane = lax.broadcasted_iota(I32, (n_keys, LANES), 1)
    low = lane < HEAD_DIM
    row = lax.broadcasted_iota(I32, (2 * CHUNK, 1), 0)
    for u in range(n_sub):
        rows = slice(u * CHUNK, (u + 1) * CHUNK)
        k_u = k_all[u * CHUNK:u * CHUNK + n_keys]
        v_u = v_all[u * CHUNK:u * CHUNK + n_keys]
        if masked:
            first = n_sub * pl.program_id(1) + u
            key_chunk = lax.broadcasted_iota(I32, (2 * CHUNK, n_keys), 1) // CHUNK
            valid = (key_chunk + first) >= 2
        for j in range(N_KV_HEADS):
            pair = slice((j // 2) * LANES, (j // 2 + 1) * LANES)
            kp = k_u[:, pair]
            vp = v_u[:, pair]
            kr = pltpu.roll(kp, HEAD_DIM, 1)
            vr = pltpu.roll(vp, HEAD_DIM, 1)
            if j % 2 == 0:
                k_low, k_high, v_low, v_high = kp, kr, vp, vr
            else:
                k_low, k_high, v_low, v_high = kr, kp, vr, vp
            q2 = jnp.concatenate([q_ref[rows, (2 * j) * LANES:(2 * j + 1) * LANES],
                                  q_ref[rows, (2 * j + 1) * LANES:(2 * j + 2) * LANES]], axis=0)
            o = jnp.zeros((2 * CHUNK, LANES), F32)
            for half, (kk, vv) in enumerate(((k_low, v_low), (k_high, v_high))):
                keep = low if half == 0 else jnp.logical_not(low)
                kz = jnp.where(keep, kk, 0.0).astype(BF16)
                vz = jnp.where(keep, vv, 0.0).astype(BF16)
                s = _dot_nt(q2, kz)
                if masked:
                    s = jnp.where(valid, s, -jnp.inf)
                sink = jnp.where(row < CHUNK, sink_ref[4 * j + half], sink_ref[4 * j + 2 + half])
                m = jnp.maximum(jnp.max(s, axis=-1, keepdims=True), sink)
                p = jnp.exp(s - m)
                p = p / (jnp.sum(p, axis=-1, keepdims=True) + jnp.exp(sink - m))
                o = o + _dot(p.astype(BF16), vz)
            o_ref[rows, (2 * j) * LANES:(2 * j + 1) * LANES] = o[:CHUNK].astype(BF16)
            o_ref[rows, (2 * j + 1) * LANES:(2 * j + 2) * LANES] = o[CHUNK:].astype(BF16)


def _attention(sinks, qr, k_srcs, v_srcs, grid, q_map, o_map, n_rows, kv_maps, kv_rows, n_sub, masked):
    assert kv_rows * len(k_srcs) == (n_sub + 2) * CHUNK

    def spec(col, fn):
        return pl.BlockSpec((kv_rows, KV_W), lambda *g: (fn(*g), col))

    in_specs = [pl.BlockSpec(memory_space=pltpu.SMEM),
                pl.BlockSpec((n_sub * CHUNK, Q_W), lambda *g: (q_map(*g), 0))]
    in_specs += [spec(col, fn) for (_, col), fn in zip(k_srcs, kv_maps)]
    in_specs += [spec(col, fn) for (_, col), fn in zip(v_srcs, kv_maps)]
    return pl.pallas_call(
        functools.partial(_attn_kernel, masked=masked, n_kv=len(k_srcs), n_sub=n_sub),
        grid=grid,
        in_specs=in_specs,
        out_specs=pl.BlockSpec((n_sub * CHUNK, Q_W), lambda *g: (o_map(*g), 0)),
        out_shape=jax.ShapeDtypeStruct((n_rows, Q_W), BF16),
        compiler_params=_cparams(("parallel",) * len(grid)),
        name="attn_prompt" if masked else "attn_sample",
    )(sinks, qr, *[a for a, _ in k_srcs], *[a for a, _ in v_srcs])


HEADS_PER_STEP = 8


def _hgrn2_chunk(r0, sl, hh, rq_ref, rf_ref, rv_ref, rg_ref, lb_ref, nw_ref, o_ref,
                 st_scr, q_scr, k_scr, v_scr, c_scr, tri, ones, rsub):
    n_sub = CHUNK // SUB
    rq = rq_ref[pl.ds(r0, CHUNK), sl]
    q = rq * _sigmoid(rq)
    k = (1.0 - lb_ref[:, sl]) * _sigmoid(-rf_ref[pl.ds(r0, CHUNK), sl])
    g = jnp.log1p(-k)
    v = rv_ref[pl.ds(r0, CHUNK), sl]
    g1 = g.astype(BF16)
    e1 = g - g1.astype(F32)
    g2 = e1.astype(BF16)
    g3 = (e1 - g2.astype(F32)).astype(BF16)
    cum = _dot(tri, g1) + _dot(tri, g2) + _dot(tri, g3)
    q_scr[hh] = q
    k_scr[hh] = k
    v_scr[hh] = v
    c_scr[hh] = cum
    st = st_scr[hh]
    inter = _dot_nt((q * jnp.exp(cum)).astype(BF16), st.astype(BF16))
    outs = []
    for i in range(n_sub):
        lo = i * SUB
        q_i = q_scr[hh, lo:lo + SUB, :]
        c_i = c_scr[hh, lo:lo + SUB, :]
        acc = inter[lo:lo + SUB, :]
        if i > 0:
            ref = c_scr[hh, lo - 1:lo, :]
            qd = (q_i * jnp.exp(c_i - ref)).astype(BF16)
            kd = (k_scr[hh, 0:lo, :] * jnp.exp(ref - c_scr[hh, 0:lo, :])).astype(BF16)
            s = _dot_nt(qd, kd)
            acc = acc + _dot(s.astype(BF16), v_scr[hh, 0:lo, :].astype(BF16))
        parts = []
        for s_ in range(SUB):
            arg = jnp.where(rsub >= s_, c_i - c_scr[hh, lo + s_:lo + s_ + 1, :], -jnp.inf)
            parts.append(q_i * k_scr[hh, lo + s_:lo + s_ + 1, :] * jnp.exp(arg))
        dsum = _dot(jnp.concatenate(parts, axis=0).astype(BF16), ones)
        for s_ in range(SUB):
            acc = acc + dsum[s_ * SUB:(s_ + 1) * SUB, :] * v_scr[hh, lo + s_:lo + s_ + 1, :]
        outs.append(acc)
    o = jnp.concatenate(outs, axis=0)
    last = c_scr[hh, CHUNK - 1:CHUNK, :]
    kdec = (k * jnp.exp(last - cum)).astype(BF16)
    st_scr[hh] = st * jnp.exp(last) + _dot(v.T.astype(BF16), kdec)
    ms = jnp.mean(o * o, axis=-1, keepdims=True)
    rg = rg_ref[pl.ds(r0, CHUNK), sl]
    o_ref[pl.ds(r0, CHUNK), sl] = (o * lax.rsqrt(ms + EPS) * nw_ref[...] * (rg * _sigmoid(rg))).astype(BF16)


def _hgrn2_kernel(rq_ref, rf_ref, rv_ref, rg_ref, lb_ref, nw_ref, s0_ref, o_ref, sfin_ref,
                  st_scr, q_scr, k_scr, v_scr, c_scr, *, n_chunks):
    t = pl.program_id(2)

    @pl.when(t == 0)
    def _():
        for hh in range(HEADS_PER_STEP):
            st_scr[hh] = s0_ref[0, hh].T

    r64 = lax.broadcasted_iota(I32, (CHUNK, CHUNK), 0)
    c64 = lax.broadcasted_iota(I32, (CHUNK, CHUNK), 1)
    tri = jnp.where(r64 >= c64, 1.0, 0.0).astype(BF16)
    ones = jnp.ones((LANES, LANES), BF16)
    rsub = lax.broadcasted_iota(I32, (SUB, LANES), 0)

    def chunk(ci, carry):
        r0 = pl.multiple_of(ci * CHUNK, CHUNK)
        for hh in range(HEADS_PER_STEP):
            sl = slice(hh * REC_D, (hh + 1) * REC_D)
            _hgrn2_chunk(r0, sl, hh, rq_ref, rf_ref, rv_ref, rg_ref, lb_ref, nw_ref, o_ref,
                         st_scr, q_scr, k_scr, v_scr, c_scr, tri, ones, rsub)
        return carry

    lax.fori_loop(0, n_chunks, chunk, 0)

    @pl.when(t == pl.num_programs(2) - 1)
    def _():
        for hh in range(HEADS_PER_STEP):
            sfin_ref[0, hh] = st_scr[hh].T


def _hgrn2(proj, lb, norm_w, s0, row_blk0, n_seq, seq_len, tb, col_blks, out_rows):
    nt = seq_len // tb
    hb = HEADS_PER_STEP
    wb = hb * REC_D
    cq, cf, cv, cg = col_blks

    def in_spec(c0):
        return pl.BlockSpec((tb, wb), lambda b, h, t: (row_blk0 + b * nt + t, c0 // wb + h))

    return pl.pallas_call(
        functools.partial(_hgrn2_kernel, n_chunks=tb // CHUNK),
        grid=(n_seq, N_REC_HEADS // hb, nt),
        in_specs=[in_spec(cq), in_spec(cf), in_spec(cv), in_spec(cg),
                  pl.BlockSpec((1, wb), lambda b, h, t: (0, h)),
                  pl.BlockSpec((1, REC_D), lambda b, h, t: (0, 0)),
                  pl.BlockSpec((1, hb, REC_D, REC_D), lambda b, h, t: (b, h, 0, 0))],
        out_specs=[pl.BlockSpec((tb, wb), lambda b, h, t: (b * nt + t, h)),
                   pl.BlockSpec((1, hb, REC_D, REC_D), lambda b, h, t: (b, h, 0, 0))],
        out_shape=[jax.ShapeDtypeStruct((out_rows, REC_W), BF16),
                   jax.ShapeDtypeStruct((n_seq, N_REC_HEADS, REC_D, REC_D), F32)],
        scratch_shapes=[pltpu.VMEM((hb, REC_D, REC_D), F32)] + [pltpu.VMEM((hb, CHUNK, REC_D), F32)] * 4,
        compiler_params=_cparams(("parallel", "parallel", "arbitrary")),
        name="hgrn2",
    )(proj, proj, proj, proj, lb, norm_w, s0)


def _pack_bf16_pairs(h):
    d = h.shape[1]
    bits = lax.bitcast_convert_type(h, jnp.uint32)
    return (bits[:, d // 2:] & jnp.uint32(0xFFFF0000)) | (bits[:, :d // 2] >> 16)


def _merge_kernel(xa_ref, xb_ref, oaa_ref, oab_ref, ora_ref, orb_ref, ga_ref, gr_ref, wa_ref, wr_ref, wo_ref,
                  nf_ref, rwh_ref, rwl_ref, rb_ref,
                  x1_ref, h2_ref, gt_ref, te_ref, lr_ref, tcnt_ref, tpre_ref, carry_scr, *, n_first):
    i = pl.program_id(0)

    @pl.when(i == 0)
    def _():
        carry_scr[...] = jnp.zeros_like(carry_scr)

    in_first = i < n_first
    a = _dot(jnp.where(in_first, oaa_ref[...], oab_ref[...]), wa_ref[...])
    r = _dot(jnp.where(in_first, ora_ref[...], orb_ref[...]), wr_ref[...])
    merged = _sigmoid(ga_ref[...]) * a + _sigmoid(gr_ref[...]) * r
    x1 = jnp.where(in_first, xa_ref[...], xb_ref[...]) + _dot(merged.astype(BF16), wo_ref[...])
    x1_ref[...] = x1
    ms = jnp.mean(x1 * x1, axis=-1, keepdims=True)
    h2 = x1 * lax.rsqrt(ms + EPS) * nf_ref[...]
    tm = h2.shape[0]
    hh, hl = _split2(h2)
    logits = _dot(hh, rwh_ref[...]) + _dot(hl, rwh_ref[...]) + _dot(hh, rwl_ref[...]) + rb_ref[...]
    lane = lax.broadcasted_iota(I32, (tm, LANES), 1)
    lane_f = lane.astype(F32)
    work = logits
    h2_ref[...] = hh
    te = jnp.zeros((tm, LANES), I32)
    tv = jnp.zeros((tm, LANES), F32)
    onehot = jnp.zeros((tm, LANES), F32)
    sels = []
    for k in range(TOP_K):
        m = jnp.max(work, axis=-1, keepdims=True)
        idx = jnp.min(jnp.where(work == m, lane_f, float(LANES)), axis=-1, keepdims=True).astype(I32)
        sel = lane == idx
        work = jnp.where(sel, -jnp.inf, work)
        onehot = onehot + jnp.where(sel, 1.0, 0.0)
        te = jnp.where(lane == k, idx, te)
        tv = jnp.where(lane == k, m, tv)
        sels.append(sel)
    ex = jnp.where(lane < TOP_K, jnp.exp(tv - tv[:, 0:1]), 0.0)
    gt_ref[...] = ex / jnp.sum(ex, axis=-1, keepdims=True)
    te_ref[...] = te
    rr = lax.broadcasted_iota(I32, (tm, tm), 0)
    cc = lax.broadcasted_iota(I32, (tm, tm), 1)
    strict = jnp.where(rr > cc, 1.0, 0.0).astype(BF16)
    lrank = _dot(strict, onehot.astype(BF16))
    lr = jnp.zeros((tm, LANES), F32)
    for k in range(TOP_K):
        lr = jnp.where(lane == k, jnp.sum(jnp.where(sels[k], lrank, 0.0), axis=-1, keepdims=True), lr)
    lr_ref[...] = lr.astype(I32)
    tcnt = jnp.sum(onehot, axis=0, keepdims=True)
    tcnt_ref[0] = tcnt.astype(I32)
    tpre_ref[0] = carry_scr[...].astype(I32)
    carry_scr[...] = carry_scr[...] + tcnt


def _merge(xa, xb, oaa, oab, ora, orb, proj, ga_blk, gr_blk, wa, wr, wo, nf, rwh, rwl, rb, tm):
    d = xa.shape[1]
    t = xa.shape[0] + xb.shape[0]
    nt = t // tm
    n_first = xa.shape[0] // tm

    def const(shape):
        return pl.BlockSpec(shape, lambda i: (0,) * len(shape), pipeline_mode=pl.Buffered(1))

    tile_vec = pl.BlockSpec((1, 1, LANES), lambda i: (i, 0, 0))
    return pl.pallas_call(
        functools.partial(_merge_kernel, n_first=n_first),
        grid=(nt,),
        in_specs=[*_two_group_specs((tm, d), n_first),
                  *_two_group_specs((tm, Q_W), n_first),
                  *_two_group_specs((tm, REC_W), n_first),
                  pl.BlockSpec((tm, d), lambda i: (i, ga_blk)),
                  pl.BlockSpec((tm, d), lambda i: (i, gr_blk)),
                  const((Q_W, d)), const((REC_W, d)), const((d, d)), const((1, d)),
                  const((d, LANES)), const((d, LANES)), const((1, LANES))],
        out_specs=[pl.BlockSpec((tm, d), lambda i: (i, 0)),
                   pl.BlockSpec((tm, d), lambda i: (i, 0)),
                   pl.BlockSpec((tm, LANES), lambda i: (i, 0)),
                   pl.BlockSpec((tm, LANES), lambda i: (i, 0)),
                   pl.BlockSpec((tm, LANES), lambda i: (i, 0)),
                   tile_vec, tile_vec],
        out_shape=[jax.ShapeDtypeStruct((t, d), F32), jax.ShapeDtypeStruct((t, d), BF16),
                   jax.ShapeDtypeStruct((t, LANES), F32), jax.ShapeDtypeStruct((t, LANES), I32),
                   jax.ShapeDtypeStruct((t, LANES), I32),
                   jax.ShapeDtypeStruct((nt, 1, LANES), I32), jax.ShapeDtypeStruct((nt, 1, LANES), I32)],
        scratch_shapes=[pltpu.VMEM((1, LANES), F32)],
        compiler_params=_cparams(("arbitrary",)),
        name="merge_router",
    )(xa, xb, oaa, oab, ora, orb, proj, proj, wa, wr, wo, nf, rwh, rwl, rb)


SUBLANES = 8
RUN_BITS = 9


def _local_rows(tm):
    return TOP_K * tm + SUBLANES * N_EXPERTS


def _run_layout(cnt_ref, glb_ref, tile, loc_scr):
    def body(e, nxt):
        j = tile * N_EXPERTS + e
        loc = nxt + ((glb_ref[j] - nxt) & (SUBLANES - 1))
        loc_scr[e] = loc
        return loc + cnt_ref[j]

    lax.fori_loop(0, N_EXPERTS, body, 0)


def _run_copies(cnt_ref, glb_ref, loc_scr, tile, make_copy, action, n_bits=RUN_BITS):
    def per_expert(e, c):
        j = tile * N_EXPERTS + e
        n = cnt_ref[j]
        glb = glb_ref[j]
        loc = loc_scr[e]
        head = jnp.minimum(n, (-glb) & (SUBLANES - 1))
        mid = ((n - head) // SUBLANES) * SUBLANES

        def single(i, c2):
            action(make_copy(loc + i, glb + i, 1))
            return c2

        lax.fori_loop(0, head, single, 0)
        for bit in range(n_bits - 1, 2, -1):
            size = 1 << bit

            @pl.when((mid & size) != 0)
            def _():
                done = head + ((mid >> (bit + 1)) << (bit + 1))
                action(make_copy(pl.multiple_of(loc + done, SUBLANES), pl.multiple_of(glb + done, SUBLANES), size))

        lax.fori_loop(head + mid, n, single, 0)
        return c

    lax.fori_loop(0, N_EXPERTS, per_expert, 0)


def _positions(te, lr, loc_scr, n_local):
    tm = te.shape[0]
    lane = lax.broadcasted_iota(I32, (1, LANES), 1)
    lvec = jnp.zeros((1, LANES), I32)
    for e in range(N_EXPERTS):
        lvec = jnp.where(lane == e, loc_scr[e], lvec)
    lvec = lvec.astype(F32)
    lane_t = lax.broadcasted_iota(I32, (tm, LANES), 1)
    pos = []
    for k in range(TOP_K):
        start = jnp.sum(jnp.where(lane_t == te[:, k:k + 1], lvec, 0.0), axis=-1, keepdims=True)
        pos.append(start.astype(I32) + lr[:, k:k + 1])
    return pos, lax.broadcasted_iota(I32, (tm, n_local), 1)


def _dispatch_kernel(cnt_ref, glb_ref, padn_ref, padlo_ref, h_ref, te_ref, lr_ref, xs_ref, srt_scr, loc_scr, sem, *,
                     pad_bits):
    tile = pl.program_id(0)
    _run_layout(cnt_ref, glb_ref, tile, loc_scr)

    @pl.when(tile == 0)
    def _():
        n_zero = 1 << (pad_bits - 1)
        srt_scr[0:n_zero, :] = jnp.zeros((n_zero, srt_scr.shape[1]), srt_scr.dtype)

        def zero_copy(loc, glb, size):
            return pltpu.make_async_copy(srt_scr.at[pl.ds(0, size)], xs_ref.at[pl.ds(glb, size)], sem)

        _run_copies(padn_ref, padlo_ref, loc_scr, 0, zero_copy, lambda cp: cp.start(), pad_bits)
        _run_copies(padn_ref, padlo_ref, loc_scr, 0, zero_copy, lambda cp: cp.wait(), pad_bits)

        def tail(action):
            def body(i, c):
                start = pl.multiple_of(padlo_ref[N_EXPERTS] + i * n_zero, SUBLANES)
                action(zero_copy(0, start, n_zero))
                return c
            lax.fori_loop(0, padn_ref[N_EXPERTS], body, 0)

        tail(lambda cp: cp.start())
        tail(lambda cp: cp.wait())

    pos, slot_iota = _positions(te_ref[...], lr_ref[...], loc_scr, srt_scr.shape[0])
    perm_t = jnp.zeros(slot_iota.shape, F32)
    for k in range(TOP_K):
        perm_t = perm_t + jnp.where(slot_iota == pos[k], 1.0, 0.0)
    rows = lax.dot_general(perm_t.astype(BF16), h_ref[...], (((0,), (0,)), ((), ())), preferred_element_type=F32)
    srt_scr[...] = _pack_bf16_pairs(rows)

    def make_copy(loc, glb, size):
        return pltpu.make_async_copy(srt_scr.at[pl.ds(loc, size)], xs_ref.at[pl.ds(glb, size)], sem)

    _run_copies(cnt_ref, glb_ref, loc_scr, tile, make_copy, lambda cp: cp.start())
    _run_copies(cnt_ref, glb_ref, loc_scr, tile, make_copy, lambda cp: cp.wait())


def _dispatch(cnt, glb, pad_n, pad_lo, h2, te, lr, n_rows, tm, bm):
    t, d = h2.shape
    assert tm < (1 << RUN_BITS)
    pad_bits = (bm - 1).bit_length()
    assert bm == 1 << pad_bits and bm // 2 <= _local_rows(tm)
    grid_spec = pltpu.PrefetchScalarGridSpec(
        num_scalar_prefetch=4,
        grid=(t // tm,),
        in_specs=[pl.BlockSpec((tm, d), lambda i, *_: (i, 0)),
                  pl.BlockSpec((tm, LANES), lambda i, *_: (i, 0)),
                  pl.BlockSpec((tm, LANES), lambda i, *_: (i, 0))],
        out_specs=pl.BlockSpec(memory_space=pl.ANY),
        scratch_shapes=[pltpu.VMEM((_local_rows(tm), d // 2), jnp.uint32), pltpu.SMEM((N_EXPERTS,), I32),
                        pltpu.SemaphoreType.DMA],
    )
    return pl.pallas_call(
        functools.partial(_dispatch_kernel, pad_bits=pad_bits),
        grid_spec=grid_spec,
        out_shape=jax.ShapeDtypeStruct((n_rows, d // 2), jnp.uint32),
        compiler_params=_cparams(("arbitrary",)),
        name="moe_dispatch",
    )(cnt, glb, pad_n, pad_lo, h2, te, lr)


MOE_SUB = 256
MOE_VMEM_LIMIT = 60 * 1024 * 1024


def _moe_kernel(be_ref, na_ref, nv_ref, x_ref, wg_ref, wu_ref, wd_ref, bgu_ref, bd_ref, o_ref, xb_scr):
    b = pl.program_id(0)
    f = pl.program_id(1)
    nf = pl.num_programs(1)
    bm = o_ref.shape[0]
    half = x_ref.shape[1]
    bg = bgu_ref[pl.ds(f, 1), :]
    bu = bgu_ref[pl.ds(nf + f, 1), :]

    def step(rows):
        @pl.when(f == 0)
        def _():
            pk = x_ref[0:rows, :]
            xb_scr[0:rows, :half] = lax.bitcast_convert_type(pk << 16, F32).astype(BF16)
            xb_scr[0:rows, half:] = lax.bitcast_convert_type(pk & jnp.uint32(0xFFFF0000), F32).astype(BF16)
            o_ref[0:rows, :] = jnp.broadcast_to(bd_ref[...], (rows, o_ref.shape[1]))
            if rows < bm:
                o_ref[rows:, :] = jnp.zeros((bm - rows, o_ref.shape[1]), F32)

        xb = xb_scr[0:rows, :]
        for s in range(wg_ref.shape[1] // MOE_SUB):
            cols = slice(s * MOE_SUB, (s + 1) * MOE_SUB)
            g = _dot(xb, wg_ref[:, cols].astype(BF16)) + bg[:, cols]
            u = _dot(xb, wu_ref[:, cols].astype(BF16)) + bu[:, cols]
            g = jnp.minimum(g, SWIGLU_LIMIT)
            u = jnp.clip(u, -SWIGLU_LIMIT, SWIGLU_LIMIT)
            act = (u + 1.0) * (g * _sigmoid(SWIGLU_ALPHA * g))
            o_ref[0:rows, :] += _dot(act.astype(BF16), wd_ref[cols, :].astype(BF16))

    active = b < na_ref[0]
    full = nv_ref[b] > bm // 2
    pl.when(jnp.logical_and(active, full))(lambda: step(bm))
    pl.when(jnp.logical_and(active, jnp.logical_not(full)))(lambda: step(bm // 2))

    @pl.when(jnp.logical_and(jnp.logical_not(active), f == 0))
    def _():
        o_ref[...] = jnp.zeros_like(o_ref)


def _moe_blocks(block_e, n_active, n_valid, xs, w_gate_up, b_gate_up, w_down, b_down, bm, tf):
    n_exp, d, two_f = w_gate_up.shape
    n_rows = xs.shape[0]
    d_ff = two_f // 2
    nb = n_rows // bm
    nf = d_ff // tf

    def blk(b, na):
        return jnp.minimum(b, na[0] - 1)

    def fi(b, f, na):
        return jnp.where(b < na[0], f, nf - 1)

    grid_spec = pltpu.PrefetchScalarGridSpec(
        num_scalar_prefetch=3,
        grid=(nb, nf),
        in_specs=[pl.BlockSpec((bm, d // 2), lambda b, f, be, na, nv: (blk(b, na), 0)),
                  pl.BlockSpec((None, d, tf), lambda b, f, be, na, nv: (be[blk(b, na)], 0, fi(b, f, na))),
                  pl.BlockSpec((None, d, tf), lambda b, f, be, na, nv: (be[blk(b, na)], 0, fi(b, f, na) + nf)),
                  pl.BlockSpec((None, tf, d), lambda b, f, be, na, nv: (be[blk(b, na)], fi(b, f, na), 0)),
                  pl.BlockSpec((None, 2 * nf, tf), lambda b, f, be, na, nv: (be[blk(b, na)], 0, 0)),
                  pl.BlockSpec((None, 1, d), lambda b, f, be, na, nv: (be[blk(b, na)], 0, 0))],
        out_specs=pl.BlockSpec((bm, d), lambda b, f, be, na, nv: (b, 0)),
        scratch_shapes=[pltpu.VMEM((bm, d), BF16)],
    )
    return pl.pallas_call(
        _moe_kernel,
        grid_spec=grid_spec,
        out_shape=jax.ShapeDtypeStruct((n_rows, d), F32),
        compiler_params=_cparams(("arbitrary", "arbitrary"), MOE_VMEM_LIMIT),
        name="moe_experts",
    )(block_e, n_active, n_valid, xs, w_gate_up, w_gate_up, w_down,
      b_gate_up.reshape(n_exp, 2 * nf, tf), b_down.reshape(n_exp, 1, d))


def _combine_kernel(cnt_ref, glb_ref, x1_ref, gt_ref, te_ref, lr_ref, os_ref, ya_ref, yb_ref, buf, loc_scr, sem, *,
                    n_first):
    tile = pl.program_id(0)

    @pl.when(tile == 0)
    def _():
        buf[...] = jnp.zeros_like(buf)

    _run_layout(cnt_ref, glb_ref, tile, loc_scr)

    def make_copy(loc, glb, size):
        return pltpu.make_async_copy(os_ref.at[pl.ds(glb, size)], buf.at[pl.ds(loc, size)], sem)

    _run_copies(cnt_ref, glb_ref, loc_scr, tile, make_copy, lambda cp: cp.start())
    pos, slot_iota = _positions(te_ref[...], lr_ref[...], loc_scr, buf.shape[0])
    gt = gt_ref[...]
    wsel = jnp.zeros(slot_iota.shape, F32)
    for k in range(TOP_K):
        wsel = wsel + jnp.where(slot_iota == pos[k], gt[:, k:k + 1], 0.0)
    _run_copies(cnt_ref, glb_ref, loc_scr, tile, make_copy, lambda cp: cp.wait())
    wh, wl = _split2(wsel)
    oh, ol = _split2(buf[...])
    y = x1_ref[...] + (_dot(wh, oh) + _dot(wl, oh) + _dot(wh, ol))

    @pl.when(tile < n_first)
    def _():
        ya_ref[...] = y

    @pl.when(tile >= n_first)
    def _():
        yb_ref[...] = y


def _combine(cnt, glb, x1, gates, te, lr, o_sorted, tm, t_first):
    t, d = x1.shape
    n_first = t_first // tm
    grid_spec = pltpu.PrefetchScalarGridSpec(
        num_scalar_prefetch=2,
        grid=(t // tm,),
        in_specs=[pl.BlockSpec((tm, d), lambda i, *_: (i, 0)),
                  pl.BlockSpec((tm, LANES), lambda i, *_: (i, 0)),
                  pl.BlockSpec((tm, LANES), lambda i, *_: (i, 0)),
                  pl.BlockSpec((tm, LANES), lambda i, *_: (i, 0)),
                  pl.BlockSpec(memory_space=pl.ANY)],
        out_specs=list(_two_group_specs((tm, d), n_first)),
        scratch_shapes=[pltpu.VMEM((_local_rows(tm), d), F32), pltpu.SMEM((N_EXPERTS,), I32),
                        pltpu.SemaphoreType.DMA],
    )
    return pl.pallas_call(
        functools.partial(_combine_kernel, n_first=n_first),
        grid_spec=grid_spec,
        out_shape=[jax.ShapeDtypeStruct((t_first, d), F32), jax.ShapeDtypeStruct((t - t_first, d), F32)],
        compiler_params=_cparams(("arbitrary",)),
        name="moe_combine",
    )(cnt, glb, x1, gates, te, lr, o_sorted)


def _pick(n, pref):
    while n % pref:
        pref //= 2
    return pref


def _forward(x_prompt, x_sample, cache_k, cache_v, state_rec, norm_mix_w, w_in, q_norm_w, k_norm_w,
             attn_sinks, rec_lb_logits, rec_norm_w, w_attn_branch, w_rec_branch, w_out, norm_ffn_w,
             router_w, router_b, w_gate_up, b_gate_up, w_down, b_down, moe_bm=1024, moe_tf=512):
    bp, sp, d = x_prompt.shape
    bs, ss, _ = x_sample.shape
    assert ss == CHUNK and sp % CHUNK == 0 and sp >= WINDOW
    tp, ts = bp * sp, bs * ss
    t = tp + ts
    xp2 = x_prompt.reshape(tp, d)
    xs2 = x_sample.reshape(ts, d)

    w = w_in[0]
    o_q, o_k, o_v, o_r = 0, Q_W, Q_W + KV_W, Q_W + 2 * KV_W
    o_g = o_r + 4 * REC_W
    w_perm = jnp.concatenate([w[:, o_g:], w[:, o_q:o_k], w[:, o_r:o_g], w[:, o_k:o_r]], axis=1).astype(BF16)
    c_aq = 2 * d
    c_rec = c_aq + Q_W
    c_k = c_rec + 4 * REC_W
    c_v = c_k + KV_W

    proj = _in_proj(xp2, xs2, norm_mix_w, w_perm, _pick(ts, 512), w_perm.shape[1] // 4)

    pos = jnp.concatenate([jnp.tile(jnp.arange(sp, dtype=I32), bp),
                           jnp.tile(PAST_LEN + jnp.arange(ss, dtype=I32), bs)]).astype(F32)
    half = HEAD_DIM // 2
    inv_freq = ROPE_THETA ** (-jnp.arange(half, dtype=F32) / half)
    ang = pos[:, None] * inv_freq[None, :]
    cos = jnp.tile(jnp.cos(ang), (1, LANES // half))
    sgn = jnp.tile(jnp.concatenate([-jnp.ones((half,), F32), jnp.ones((half,), F32)]), LANES // HEAD_DIM)
    sin = jnp.tile(jnp.sin(ang), (1, LANES // half)) * sgn[None, :]
    qw = jnp.tile(q_norm_w[0], LANES // HEAD_DIM)[None, :]
    kw = jnp.tile(k_norm_w[0], LANES // HEAD_DIM)[None, :]
    qr, kr = _qk_prep(proj, cos, sin, qw, kw, c_aq // Q_W, c_k // KV_W, _pick(ts, 512))

    sinks = attn_sinks[0]
    npc = sp // CHUNK
    vcol = c_v // KV_W

    assert npc % 2 == 0
    npp = npc // 2

    def prow(back):
        return lambda b, c: b * npp + jnp.maximum(c - back, 0)

    oa_p = _attention(sinks, qr, [(kr, 0)] * 2, [(proj, vcol)] * 2, (bp, npp),
                      prow(0), prow(0), tp, [prow(1), prow(0)], 2 * CHUNK, 2, True)
    ck = cache_k[0].reshape(bs * WINDOW, KV_W)
    cv = cache_v[0].reshape(bs * WINDOW, KV_W)
    srow = tp // CHUNK
    oa_s = _attention(sinks, qr, [(ck, 0), (ck, 0), (kr, 0)], [(cv, 0), (cv, 0), (proj, vcol)], (bs,),
                      lambda b: srow + b, lambda b: b, ts,
                      [lambda b: 2 * b, lambda b: 2 * b + 1, lambda b: srow + b], CHUNK, 1, False)

    lb = jax.nn.softmax(rec_lb_logits.astype(F32), axis=0)[0][None, :]
    rec_cols = tuple(c_rec + i * REC_W for i in range(4))
    nw_rec = rec_norm_w[0][None, :]
    tb = _pick(sp, 512)
    or_p, sfin_p = _hgrn2(proj, lb, nw_rec, jnp.zeros((bp, N_REC_HEADS, REC_D, REC_D), F32),
                          0, bp, sp, tb, rec_cols, tp)
    or_s, sfin_s = _hgrn2(proj, lb, nw_rec, state_rec[0], tp // CHUNK, bs, ss, CHUNK, rec_cols, ts)

    rw = jnp.pad(router_w[0], ((0, 0), (0, LANES - N_EXPERTS)))
    rwh = rw.astype(BF16)
    rwl = (rw - rwh.astype(F32)).astype(BF16)
    rb = jnp.concatenate([router_b[0].astype(F32), jnp.full((LANES - N_EXPERTS,), -jnp.inf, F32)])[None, :]
    tm = _pick(ts, 256)
    x1, h2, gt, te, lr, tcnt, tpre = _merge(xp2, xs2, oa_p, oa_s, or_p, or_s, proj, 0, 1,
                                            w_attn_branch[0].astype(BF16),
                                            w_rec_branch[0].astype(BF16), w_out[0].astype(BF16),
                                            norm_ffn_w, rwh, rwl, rb, tm)

    tcnt = tcnt[:, 0, :N_EXPERTS]
    counts = jnp.sum(tcnt, axis=0)
    padded = (counts + moe_bm - 1) // moe_bm * moe_bm
    pad_end = jnp.cumsum(padded)
    pad_start = pad_end - padded
    nb = (t * TOP_K) // moe_bm + N_EXPERTS
    block_start = jnp.arange(nb, dtype=I32) * moe_bm
    block_e = jnp.minimum(jnp.sum((pad_end[None, :] <= block_start[:, None]).astype(I32), axis=1), N_EXPERTS - 1)
    n_active = (pad_end[-1:] // moe_bm).astype(I32)
    n_valid = jnp.clip((pad_start + counts)[block_e] - block_start, 0, moe_bm).astype(I32)
    run_cnt = tcnt.reshape(-1)
    run_glb = (pad_start[None, :] + tpre[:, 0, :N_EXPERTS]).reshape(-1)

    pad_n = jnp.concatenate([padded - counts, (nb * moe_bm - pad_end[-1:]) // (moe_bm // 2)]).astype(I32)
    pad_lo = jnp.concatenate([pad_start + counts, pad_end[-1:]]).astype(I32)
    xs = _dispatch(run_cnt, run_glb, pad_n, pad_lo, h2, te, lr, nb * moe_bm, tm, moe_bm)
    o_sorted = _moe_blocks(block_e, n_active, n_valid, xs, w_gate_up[0], b_gate_up[0], w_down[0], b_down[0],
                           moe_bm, moe_tf)
    y_p, y_s = _combine(run_cnt, run_glb, x1, gt, te, lr, o_sorted, tm, tp)
    y_p = y_p.reshape(bp, sp, d)
    y_s = y_s.reshape(bs, ss, d)
    v_new = proj[:, c_v:c_v + KV_W]
    kp = kr[:tp].reshape(bp, sp, N_KV_HEADS, HEAD_DIM)[:, -WINDOW:]
    vp = v_new[:tp].reshape(bp, sp, N_KV_HEADS, HEAD_DIM)[:, -WINDOW:]
    ks_new = kr[tp:].reshape(bs, ss, N_KV_HEADS, HEAD_DIM)
    vs_new = v_new[tp:].reshape(bs, ss, N_KV_HEADS, HEAD_DIM)
    ks = jnp.concatenate([cache_k[0], ks_new], axis=1)[:, -WINDOW:]
    vs = jnp.concatenate([cache_v[0], vs_new], axis=1)[:, -WINDOW:]
    return (y_p, y_s, kp[None], vp[None], sfin_p[None], ks[None], vs[None], sfin_s[None])


def kernel(x_prompt, x_sample, cache_k, cache_v, state_rec, norm_mix_w, w_in, q_norm_w, k_norm_w, attn_sinks, rec_lb_logits, rec_norm_w, w_attn_branch, w_rec_branch, w_out, norm_ffn_w, router_w, router_b, w_gate_up, b_gate_up, w_down, b_down):
    return _forward(x_prompt, x_sample, cache_k, cache_v, state_rec, norm_mix_w, w_in, q_norm_w, k_norm_w,
                    attn_sinks, rec_lb_logits, rec_norm_w, w_attn_branch, w_rec_branch, w_out, norm_ffn_w,
                    router_w, router_b, w_gate_up, b_gate_up, w_down, b_down)
```

```python
import functools

import jax
import jax.numpy as jnp
from jax import lax
from jax.experimental import pallas as pl
from jax.experimental.pallas import tpu as pltpu

F32 = jnp.float32
BF16 = jnp.bfloat16
I32 = jnp.int32

CHUNK = 64
N_HEADS = 16
N_KV_HEADS = 4
HEAD_DIM = 64
WINDOW = 128
ROPE_THETA = 10000.0
N_REC_HEADS = 8
REC_D = 128
N_EXPERTS = 32
TOP_K = 4
SWIGLU_LIMIT = 7.0
SWIGLU_ALPHA = 1.702
EPS = 1e-6
PAST_LEN = 2048
LANES = 128
SUB = 16

VMEM_LIMIT = 56 * 1024 * 1024

Q_W = N_HEADS * HEAD_DIM
KV_W = N_KV_HEADS * HEAD_DIM
REC_W = N_REC_HEADS * REC_D


def _cparams(sem, vmem_limit=VMEM_LIMIT):
    return pltpu.CompilerParams(dimension_semantics=sem, vmem_limit_bytes=vmem_limit)


def _sigmoid(x):
    return 1.0 / (1.0 + jnp.exp(-x))


def _split2(x):
    hi = x.astype(BF16)
    lo = (x - hi.astype(F32)).astype(BF16)
    return hi, lo


def _dot(a, b):
    return jnp.dot(a, b, preferred_element_type=F32)


def _dot_nt(a, b):
    return lax.dot_general(a, b, (((1,), (1,)), ((), ())), preferred_element_type=F32)


def _two_group_specs(shape, n_first):
    first = pl.BlockSpec(shape, lambda i, *_: (jnp.minimum(i, n_first - 1), 0))
    second = pl.BlockSpec(shape, lambda i, *_: (jnp.maximum(i - n_first, 0), 0))
    return first, second


def _in_proj_kernel(xa_ref, xb_ref, nw_ref, w_ref, o_ref, h_scr, *, n_first):
    def norm(x_ref):
        x = x_ref[...]
        ms = jnp.mean(x * x, axis=-1, keepdims=True)
        h_scr[...] = (x * lax.rsqrt(ms + EPS) * nw_ref[...]).astype(BF16)

    first_col = pl.program_id(1) == 0
    in_first = pl.program_id(0) < n_first
    pl.when(jnp.logical_and(first_col, in_first))(lambda: norm(xa_ref))
    pl.when(jnp.logical_and(first_col, jnp.logical_not(in_first)))(lambda: norm(xb_ref))
    o_ref[...] = _dot(h_scr[...], w_ref[...])


def _in_proj(xa, xb, norm_w, w, tm, tn):
    d = xa.shape[1]
    t = xa.shape[0] + xb.shape[0]
    n = w.shape[1]
    n_first = xa.shape[0] // tm
    return pl.pallas_call(
        functools.partial(_in_proj_kernel, n_first=n_first),
        grid=(t // tm, n // tn),
        in_specs=[*_two_group_specs((tm, d), n_first),
                  pl.BlockSpec((1, d), lambda i, j: (0, 0)),
                  pl.BlockSpec((d, tn), lambda i, j: (0, j))],
        out_specs=pl.BlockSpec((tm, tn), lambda i, j: (i, j)),
        out_shape=jax.ShapeDtypeStruct((t, n), F32),
        scratch_shapes=[pltpu.VMEM((tm, d), BF16)],
        compiler_params=_cparams(("parallel", "arbitrary")),
        name="in_proj",
    )(xa, xb, norm_w, w)


def _norm_rope(x, nw, cos, sin, bd):
    hi, lo = _split2(x * x)
    ss = _dot(hi, bd) + _dot(lo, bd)
    y = x * lax.rsqrt(ss * (1.0 / HEAD_DIM) + EPS) * nw
    lane = lax.broadcasted_iota(I32, y.shape, 1)
    first_half = (lane % HEAD_DIM) < (HEAD_DIM // 2)
    rot = jnp.where(first_half, pltpu.roll(y, LANES - HEAD_DIM // 2, 1), pltpu.roll(y, HEAD_DIM // 2, 1))
    return y * cos + rot * sin


def _qk_prep_kernel(q_ref, k_ref, cos_ref, sin_ref, qw_ref, kw_ref, qo_ref, ko_ref):
    r = lax.broadcasted_iota(I32, (LANES, LANES), 0) // HEAD_DIM
    c = lax.broadcasted_iota(I32, (LANES, LANES), 1) // HEAD_DIM
    bd = jnp.where(r == c, 1.0, 0.0).astype(BF16)
    cos = cos_ref[...]
    sin = sin_ref[...]
    scale = HEAD_DIM ** -0.5
    for g in range(Q_W // LANES):
        sl = slice(g * LANES, (g + 1) * LANES)
        qo_ref[:, sl] = (_norm_rope(q_ref[:, sl], qw_ref[...], cos, sin, bd) * scale).astype(BF16)
    for g in range(KV_W // LANES):
        sl = slice(g * LANES, (g + 1) * LANES)
        ko_ref[:, sl] = _norm_rope(k_ref[:, sl], kw_ref[...], cos, sin, bd)


def _qk_prep(proj, cos, sin, qw, kw, q_blk, k_blk, tr):
    t = proj.shape[0]
    return pl.pallas_call(
        _qk_prep_kernel,
        grid=(t // tr,),
        in_specs=[pl.BlockSpec((tr, Q_W), lambda i: (i, q_blk)),
                  pl.BlockSpec((tr, KV_W), lambda i: (i, k_blk)),
                  pl.BlockSpec((tr, LANES), lambda i: (i, 0)),
                  pl.BlockSpec((tr, LANES), lambda i: (i, 0)),
                  pl.BlockSpec((1, LANES), lambda i: (0, 0)),
                  pl.BlockSpec((1, LANES), lambda i: (0, 0))],
        out_specs=[pl.BlockSpec((tr, Q_W), lambda i: (i, 0)),
                   pl.BlockSpec((tr, KV_W), lambda i: (i, 0))],
        out_shape=[jax.ShapeDtypeStruct((t, Q_W), BF16), jax.ShapeDtypeStruct((t, KV_W), F32)],
        compiler_params=_cparams(("parallel",)),
        name="qk_prep",
    )(proj, proj, cos, sin, qw, kw)


def _attn_kernel(sink_ref, q_ref, *refs, masked, n_kv, n_sub):
    k_all = jnp.concatenate([r[...] for r in refs[:n_kv]], axis=0)
    v_all = jnp.concatenate([r[...] for r in refs[n_kv:2 * n_kv]], axis=0)
    o_ref = refs[2 * n_kv]
    n_keys = 3 * CHUNK
    lane = lax.broadcasted_iota(I32, (n_keys, LANES), 1)
    low = lane < HEAD_DIM
    row = lax.broadcasted_iota(I32, (2 * CHUNK, 1), 0)
    for u in range(n_sub):
        rows = slice(u * CHUNK, (u + 1) * CHUNK)
        k_u = k_all[u * CHUNK:u * CHUNK + n_keys]
        v_u = v_all[u * CHUNK:u * CHUNK + n_keys]
        if masked:
            first = n_sub * pl.program_id(1) + u
            key_chunk = lax.broadcasted_iota(I32, (2 * CHUNK, n_keys), 1) // CHUNK
            valid = (key_chunk + first) >= 2
        for j in range(N_KV_HEADS):
            pair = slice((j // 2) * LANES, (j // 2 + 1) * LANES)
            kp = k_u[:, pair]
            vp = v_u[:, pair]
            kr = pltpu.roll(kp, HEAD_DIM, 1)
            vr = pltpu.roll(vp, HEAD_DIM, 1)
            if j % 2 == 0:
                k_low, k_high, v_low, v_high = kp, kr, vp, vr
            else:
                k_low, k_high, v_low, v_high = kr, kp, vr, vp
            q2 = jnp.concatenate([q_ref[rows, (2 * j) * LANES:(2 * j + 1) * LANES],
                                  q_ref[rows, (2 * j + 1) * LANES:(2 * j + 2) * LANES]], axis=0)
            o = jnp.zeros((2 * CHUNK, LANES), F32)
            for half, (kk, vv) in enumerate(((k_low, v_low), (k_high, v_high))):
                keep = low if half == 0 else jnp.logical_not(low)
                kz = jnp.where(keep, kk, 0.0).astype(BF16)
                vz = jnp.where(keep, vv, 0.0).astype(BF16)
                s = _dot_nt(q2, kz)
                if masked:
                    s = jnp.where(valid, s, -jnp.inf)
                sink = jnp.where(row < CHUNK, sink_ref[4 * j + half], sink_ref[4 * j + 2 + half])
                m = jnp.maximum(jnp.max(s, axis=-1, keepdims=True), sink)
                p = jnp.exp(s - m)
                p = p / (jnp.sum(p, axis=-1, keepdims=True) + jnp.exp(sink - m))
                o = o + _dot(p.astype(BF16), vz)
            o_ref[rows, (2 * j) * LANES:(2 * j + 1) * LANES] = o[:CHUNK].astype(BF16)
            o_ref[rows, (2 * j + 1) * LANES:(2 * j + 2) * LANES] = o[CHUNK:].astype(BF16)


def _attention(sinks, qr, k_srcs, v_srcs, grid, q_map, o_map, n_rows, kv_maps, kv_rows, n_sub, masked):
    assert kv_rows * len(k_srcs) == (n_sub + 2) * CHUNK

    def spec(col, fn):
        return pl.BlockSpec((kv_rows, KV_W), lambda *g: (fn(*g), col))

    in_specs = [pl.BlockSpec(memory_space=pltpu.SMEM),
                pl.BlockSpec((n_sub * CHUNK, Q_W), lambda *g: (q_map(*g), 0))]
    in_specs += [spec(col, fn) for (_, col), fn in zip(k_srcs, kv_maps)]
    in_specs += [spec(col, fn) for (_, col), fn in zip(v_srcs, kv_maps)]
    return pl.pallas_call(
        functools.partial(_attn_kernel, masked=masked, n_kv=len(k_srcs), n_sub=n_sub),
        grid=grid,
        in_specs=in_specs,
        out_specs=pl.BlockSpec((n_sub * CHUNK, Q_W), lambda *g: (o_map(*g), 0)),
        out_shape=jax.ShapeDtypeStruct((n_rows, Q_W), BF16),
        compiler_params=_cparams(("parallel",) * len(grid)),
        name="attn_prompt" if masked else "attn_sample",
    )(sinks, qr, *[a for a, _ in k_srcs], *[a for a, _ in v_srcs])


HEADS_PER_STEP = 8


def _hgrn2_chunk(r0, sl, hh, rq_ref, rf_ref, rv_ref, rg_ref, lb_ref, nw_ref, o_ref,
                 st_scr, q_scr, k_scr, v_scr, c_scr, tri, ones, rsub):
    n_sub = CHUNK // SUB
    rq = rq_ref[pl.ds(r0, CHUNK), sl]
    q = rq * _sigmoid(rq)
    k = (1.0 - lb_ref[:, sl]) * _sigmoid(-rf_ref[pl.ds(r0, CHUNK), sl])
    g = jnp.log1p(-k)
    v = rv_ref[pl.ds(r0, CHUNK), sl]
    g1 = g.astype(BF16)
    e1 = g - g1.astype(F32)
    g2 = e1.astype(BF16)
    g3 = (e1 - g2.astype(F32)).astype(BF16)
    cum = _dot(tri, g1) + _dot(tri, g2) + _dot(tri, g3)
    q_scr[hh] = q
    k_scr[hh] = k
    v_scr[hh] = v
    c_scr[hh] = cum
    st = st_scr[hh]
    inter = _dot_nt((q * jnp.exp(cum)).astype(BF16), st.astype(BF16))
    outs = []
    for i in range(n_sub):
        lo = i * SUB
        q_i = q_scr[hh, lo:lo + SUB, :]
        c_i = c_scr[hh, lo:lo + SUB, :]
        acc = inter[lo:lo + SUB, :]
        if i > 0:
            ref = c_scr[hh, lo - 1:lo, :]
            qd = (q_i * jnp.exp(c_i - ref)).astype(BF16)
            kd = (k_scr[hh, 0:lo, :] * jnp.exp(ref - c_scr[hh, 0:lo, :])).astype(BF16)
            s = _dot_nt(qd, kd)
            acc = acc + _dot(s.astype(BF16), v_scr[hh, 0:lo, :].astype(BF16))
        parts = []
        for s_ in range(SUB):
            arg = jnp.where(rsub >= s_, c_i - c_scr[hh, lo + s_:lo + s_ + 1, :], -jnp.inf)
            parts.append(q_i * k_scr[hh, lo + s_:lo + s_ + 1, :] * jnp.exp(arg))
        dsum = _dot(jnp.concatenate(parts, axis=0).astype(BF16), ones)
        for s_ in range(SUB):
            acc = acc + dsum[s_ * SUB:(s_ + 1) * SUB, :] * v_scr[hh, lo + s_:lo + s_ + 1, :]
        outs.append(acc)
    o = jnp.concatenate(outs, axis=0)
    last = c_scr[hh, CHUNK - 1:CHUNK, :]
    kdec = (k * jnp.exp(last - cum)).astype(BF16)
    st_scr[hh] = st * jnp.exp(last) + _dot(v.T.astype(BF16), kdec)
    ms = jnp.mean(o * o, axis=-1, keepdims=True)
    rg = rg_ref[pl.ds(r0, CHUNK), sl]
    o_ref[pl.ds(r0, CHUNK), sl] = (o * lax.rsqrt(ms + EPS) * nw_ref[...] * (rg * _sigmoid(rg))).astype(BF16)


def _hgrn2_kernel(rq_ref, rf_ref, rv_ref, rg_ref, lb_ref, nw_ref, s0_ref, o_ref, sfin_ref,
                  st_scr, q_scr, k_scr, v_scr, c_scr, *, n_chunks):
    t = pl.program_id(2)

    @pl.when(t == 0)
    def _():
        for hh in range(HEADS_PER_STEP):
            st_scr[hh] = s0_ref[0, hh].T

    r64 = lax.broadcasted_iota(I32, (CHUNK, CHUNK), 0)
    c64 = lax.broadcasted_iota(I32, (CHUNK, CHUNK), 1)
    tri = jnp.where(r64 >= c64, 1.0, 0.0).astype(BF16)
    ones = jnp.ones((LANES, LANES), BF16)
    rsub = lax.broadcasted_iota(I32, (SUB, LANES), 0)

    def chunk(ci, carry):
        r0 = pl.multiple_of(ci * CHUNK, CHUNK)
        for hh in range(HEADS_PER_STEP):
            sl = slice(hh * REC_D, (hh + 1) * REC_D)
            _hgrn2_chunk(r0, sl, hh, rq_ref, rf_ref, rv_ref, rg_ref, lb_ref, nw_ref, o_ref,
                         st_scr, q_scr, k_scr, v_scr, c_scr, tri, ones, rsub)
        return carry

    lax.fori_loop(0, n_chunks, chunk, 0)

    @pl.when(t == pl.num_programs(2) - 1)
    def _():
        for hh in range(HEADS_PER_STEP):
            sfin_ref[0, hh] = st_scr[hh].T


def _hgrn2(proj, lb, norm_w, s0, row_blk0, n_seq, seq_len, tb, col_blks, out_rows):
    nt = seq_len // tb
    hb = HEADS_PER_STEP
    wb = hb * REC_D
    cq, cf, cv, cg = col_blks

    def in_spec(c0):
        return pl.BlockSpec((tb, wb), lambda b, h, t: (row_blk0 + b * nt + t, c0 // wb + h))

    return pl.pallas_call(
        functools.partial(_hgrn2_kernel, n_chunks=tb // CHUNK),
        grid=(n_seq, N_REC_HEADS // hb, nt),
        in_specs=[in_spec(cq), in_spec(cf), in_spec(cv), in_spec(cg),
                  pl.BlockSpec((1, wb), lambda b, h, t: (0, h)),
                  pl.BlockSpec((1, REC_D), lambda b, h, t: (0, 0)),
                  pl.BlockSpec((1, hb, REC_D, REC_D), lambda b, h, t: (b, h, 0, 0))],
        out_specs=[pl.BlockSpec((tb, wb), lambda b, h, t: (b * nt + t, h)),
                   pl.BlockSpec((1, hb, REC_D, REC_D), lambda b, h, t: (b, h, 0, 0))],
        out_shape=[jax.ShapeDtypeStruct((out_rows, REC_W), BF16),
                   jax.ShapeDtypeStruct((n_seq, N_REC_HEADS, REC_D, REC_D), F32)],
        scratch_shapes=[pltpu.VMEM((hb, REC_D, REC_D), F32)] + [pltpu.VMEM((hb, CHUNK, REC_D), F32)] * 4,
        compiler_params=_cparams(("parallel", "parallel", "arbitrary")),
        name="hgrn2",
    )(proj, proj, proj, proj, lb, norm_w, s0)


def _pack_bf16_pairs(h):
    d = h.shape[1]
    bits = lax.bitcast_convert_type(h, jnp.uint32)
    return (bits[:, d // 2:] & jnp.uint32(0xFFFF0000)) | (bits[:, :d // 2] >> 16)


def _merge_kernel(xa_ref, xb_ref, oaa_ref, oab_ref, ora_ref, orb_ref, ga_ref, gr_ref, wa_ref, wr_ref, wo_ref,
                  nf_ref, rwh_ref, rwl_ref, rb_ref,
                  x1_ref, h2_ref, gt_ref, te_ref, lr_ref, tcnt_ref, tpre_ref, carry_scr, *, n_first):
    i = pl.program_id(0)

    @pl.when(i == 0)
    def _():
        carry_scr[...] = jnp.zeros_like(carry_scr)

    in_first = i < n_first
    a = _dot(jnp.where(in_first, oaa_ref[...], oab_ref[...]), wa_ref[...])
    r = _dot(jnp.where(in_first, ora_ref[...], orb_ref[...]), wr_ref[...])
    merged = _sigmoid(ga_ref[...]) * a + _sigmoid(gr_ref[...]) * r
    x1 = jnp.where(in_first, xa_ref[...], xb_ref[...]) + _dot(merged.astype(BF16), wo_ref[...])
    x1_ref[...] = x1
    ms = jnp.mean(x1 * x1, axis=-1, keepdims=True)
    h2 = x1 * lax.rsqrt(ms + EPS) * nf_ref[...]
    tm = h2.shape[0]
    hh, hl = _split2(h2)
    logits = _dot(hh, rwh_ref[...]) + _dot(hl, rwh_ref[...]) + _dot(hh, rwl_ref[...]) + rb_ref[...]
    lane = lax.broadcasted_iota(I32, (tm, LANES), 1)
    lane_f = lane.astype(F32)
    work = logits
    h2_ref[...] = hh
    te = jnp.zeros((tm, LANES), I32)
    tv = jnp.zeros((tm, LANES), F32)
    onehot = jnp.zeros((tm, LANES), F32)
    sels = []
    for k in range(TOP_K):
        m = jnp.max(work, axis=-1, keepdims=True)
        idx = jnp.min(jnp.where(work == m, lane_f, float(LANES)), axis=-1, keepdims=True).astype(I32)
        sel = lane == idx
        work = jnp.where(sel, -jnp.inf, work)
        onehot = onehot + jnp.where(sel, 1.0, 0.0)
        te = jnp.where(lane == k, idx, te)
        tv = jnp.where(lane == k, m, tv)
        sels.append(sel)
    ex = jnp.where(lane < TOP_K, jnp.exp(tv - tv[:, 0:1]), 0.0)
    gt_ref[...] = ex / jnp.sum(ex, axis=-1, keepdims=True)
    te_ref[...] = te
    rr = lax.broadcasted_iota(I32, (tm, tm), 0)
    cc = lax.broadcasted_iota(I32, (tm, tm), 1)
    strict = jnp.where(rr > cc, 1.0, 0.0).astype(BF16)
    lrank = _dot(strict, onehot.astype(BF16))
    lr = jnp.zeros((tm, LANES), F32)
    for k in range(TOP_K):
        lr = jnp.where(lane == k, jnp.sum(jnp.where(sels[k], lrank, 0.0), axis=-1, keepdims=True), lr)
    lr_ref[...] = lr.astype(I32)
    tcnt = jnp.sum(onehot, axis=0, keepdims=True)
    tcnt_ref[0] = tcnt.astype(I32)
    tpre_ref[0] = carry_scr[...].astype(I32)
    carry_scr[...] = carry_scr[...] + tcnt


def _merge(xa, xb, oaa, oab, ora, orb, proj, ga_blk, gr_blk, wa, wr, wo, nf, rwh, rwl, rb, tm):
    d = xa.shape[1]
    t = xa.shape[0] + xb.shape[0]
    nt = t // tm
    n_first = xa.shape[0] // tm

    def const(shape):
        return pl.BlockSpec(shape, lambda i: (0,) * len(shape), pipeline_mode=pl.Buffered(1))

    tile_vec = pl.BlockSpec((1, 1, LANES), lambda i: (i, 0, 0))
    return pl.pallas_call(
        functools.partial(_merge_kernel, n_first=n_first),
        grid=(nt,),
        in_specs=[*_two_group_specs((tm, d), n_first),
                  *_two_group_specs((tm, Q_W), n_first),
                  *_two_group_specs((tm, REC_W), n_first),
                  pl.BlockSpec((tm, d), lambda i: (i, ga_blk)),
                  pl.BlockSpec((tm, d), lambda i: (i, gr_blk)),
                  const((Q_W, d)), const((REC_W, d)), const((d, d)), const((1, d)),
                  const((d, LANES)), const((d, LANES)), const((1, LANES))],
        out_specs=[pl.BlockSpec((tm, d), lambda i: (i, 0)),
                   pl.BlockSpec((tm, d), lambda i: (i, 0)),
                   pl.BlockSpec((tm, LANES), lambda i: (i, 0)),
                   pl.BlockSpec((tm, LANES), lambda i: (i, 0)),
                   pl.BlockSpec((tm, LANES), lambda i: (i, 0)),
                   tile_vec, tile_vec],
        out_shape=[jax.ShapeDtypeStruct((t, d), F32), jax.ShapeDtypeStruct((t, d), BF16),
                   jax.ShapeDtypeStruct((t, LANES), F32), jax.ShapeDtypeStruct((t, LANES), I32),
                   jax.ShapeDtypeStruct((t, LANES), I32),
                   jax.ShapeDtypeStruct((nt, 1, LANES), I32), jax.ShapeDtypeStruct((nt, 1, LANES), I32)],
        scratch_shapes=[pltpu.VMEM((1, LANES), F32)],
        compiler_params=_cparams(("arbitrary",)),
        name="merge_router",
    )(xa, xb, oaa, oab, ora, orb, proj, proj, wa, wr, wo, nf, rwh, rwl, rb)


SUBLANES = 8
RUN_BITS = 9


def _local_rows(tm):
    return TOP_K * tm + SUBLANES * N_EXPERTS


def _run_layout(cnt_ref, glb_ref, tile, loc_scr):
    def body(e, nxt):
        j = tile * N_EXPERTS + e
        loc = nxt + ((glb_ref[j] - nxt) & (SUBLANES - 1))
        loc_scr[e] = loc
        return loc + cnt_ref[j]

    lax.fori_loop(0, N_EXPERTS, body, 0)


def _run_copies(cnt_ref, glb_ref, loc_scr, tile, make_copy, action, n_bits=RUN_BITS):
    def per_expert(e, c):
        j = tile * N_EXPERTS + e
        n = cnt_ref[j]
        glb = glb_ref[j]
        loc = loc_scr[e]
        head = jnp.minimum(n, (-glb) & (SUBLANES - 1))
        mid = ((n - head) // SUBLANES) * SUBLANES

        def single(i, c2):
            action(make_copy(loc + i, glb + i, 1))
            return c2

        lax.fori_loop(0, head, single, 0)

        def pieces(bits):
            for bit in bits:
                size = 1 << bit

                @pl.when((mid & size) != 0)
                def _():
                    done = head + ((mid >> (bit + 1)) << (bit + 1))
                    action(make_copy(pl.multiple_of(loc + done, SUBLANES), pl.multiple_of(glb + done, SUBLANES),
                                     size))

        split = min(n_bits, 6)
        pl.when(mid >= (1 << split))(lambda: pieces(range(n_bits - 1, split - 1, -1)))
        pieces(range(split - 1, 2, -1))
        lax.fori_loop(head + mid, n, single, 0)
        return c

    lax.fori_loop(0, N_EXPERTS, per_expert, 0)


def _positions(te, lr, loc_scr, n_local):
    tm = te.shape[0]
    lane = lax.broadcasted_iota(I32, (1, LANES), 1)
    lvec = jnp.zeros((1, LANES), I32)
    for e in range(N_EXPERTS):
        lvec = jnp.where(lane == e, loc_scr[e], lvec)
    lvec = lvec.astype(F32)
    lane_t = lax.broadcasted_iota(I32, (tm, LANES), 1)
    pos = []
    for k in range(TOP_K):
        start = jnp.sum(jnp.where(lane_t == te[:, k:k + 1], lvec, 0.0), axis=-1, keepdims=True)
        pos.append(start.astype(I32) + lr[:, k:k + 1])
    return pos, lax.broadcasted_iota(I32, (tm, n_local), 1)


def _dispatch_kernel(cnt_ref, glb_ref, padn_ref, padlo_ref, h_ref, te_ref, lr_ref, xs_ref, srt_scr, loc_scr, sem, *,
                     pad_bits):
    tile = pl.program_id(0)
    _run_layout(cnt_ref, glb_ref, tile, loc_scr)

    @pl.when(tile == 0)
    def _():
        n_zero = 1 << (pad_bits - 1)
        srt_scr[0:n_zero, :] = jnp.zeros((n_zero, srt_scr.shape[1]), srt_scr.dtype)

        def zero_copy(loc, glb, size):
            return pltpu.make_async_copy(srt_scr.at[pl.ds(0, size)], xs_ref.at[pl.ds(glb, size)], sem)

        _run_copies(padn_ref, padlo_ref, loc_scr, 0, zero_copy, lambda cp: cp.start(), pad_bits)
        _run_copies(padn_ref, padlo_ref, loc_scr, 0, zero_copy, lambda cp: cp.wait(), pad_bits)

        def tail(action):
            def body(i, c):
                start = pl.multiple_of(padlo_ref[N_EXPERTS] + i * n_zero, SUBLANES)
                action(zero_copy(0, start, n_zero))
                return c
            lax.fori_loop(0, padn_ref[N_EXPERTS], body, 0)

        tail(lambda cp: cp.start())
        tail(lambda cp: cp.wait())

    pos, slot_iota = _positions(te_ref[...], lr_ref[...], loc_scr, srt_scr.shape[0])
    perm_t = jnp.zeros(slot_iota.shape, F32)
    for k in range(TOP_K):
        perm_t = perm_t + jnp.where(slot_iota == pos[k], 1.0, 0.0)
    rows = lax.dot_general(perm_t.astype(BF16), h_ref[...], (((0,), (0,)), ((), ())), preferred_element_type=F32)
    srt_scr[...] = _pack_bf16_pairs(rows)

    def make_copy(loc, glb, size):
        return pltpu.make_async_copy(srt_scr.at[pl.ds(loc, size)], xs_ref.at[pl.ds(glb, size)], sem)

    _run_copies(cnt_ref, glb_ref, loc_scr, tile, make_copy, lambda cp: cp.start())
    _run_copies(cnt_ref, glb_ref, loc_scr, tile, make_copy, lambda cp: cp.wait())


def _dispatch(cnt, glb, pad_n, pad_lo, h2, te, lr, n_rows, tm, bm):
    t, d = h2.shape
    assert tm < (1 << RUN_BITS)
    pad_bits = (bm - 1).bit_length()
    assert bm == 1 << pad_bits and bm // 2 <= _local_rows(tm)
    grid_spec = pltpu.PrefetchScalarGridSpec(
        num_scalar_prefetch=4,
        grid=(t // tm,),
        in_specs=[pl.BlockSpec((tm, d), lambda i, *_: (i, 0)),
                  pl.BlockSpec((tm, LANES), lambda i, *_: (i, 0)),
                  pl.BlockSpec((tm, LANES), lambda i, *_: (i, 0))],
        out_specs=pl.BlockSpec(memory_space=pl.ANY),
        scratch_shapes=[pltpu.VMEM((_local_rows(tm), d // 2), jnp.uint32), pltpu.SMEM((N_EXPERTS,), I32),
                        pltpu.SemaphoreType.DMA],
    )
    return pl.pallas_call(
        functools.partial(_dispatch_kernel, pad_bits=pad_bits),
        grid_spec=grid_spec,
        out_shape=jax.ShapeDtypeStruct((n_rows, d // 2), jnp.uint32),
        compiler_params=_cparams(("arbitrary",)),
        name="moe_dispatch",
    )(cnt, glb, pad_n, pad_lo, h2, te, lr)


MOE_SUB = 256
MOE_VMEM_LIMIT = 60 * 1024 * 1024


def _moe_kernel(be_ref, na_ref, nv_ref, x_ref, wg_ref, wu_ref, wd_ref, bgu_ref, bd_ref, o_ref, xb_scr):
    b = pl.program_id(0)
    f = pl.program_id(1)
    nf = pl.num_programs(1)
    bm = o_ref.shape[0]
    half = x_ref.shape[1]
    bg = bgu_ref[pl.ds(f, 1), :]
    bu = bgu_ref[pl.ds(nf + f, 1), :]

    def step(rows):
        @pl.when(f == 0)
        def _():
            pk = x_ref[0:rows, :]
            xb_scr[0:rows, :half] = lax.bitcast_convert_type(pk << 16, F32).astype(BF16)
            xb_scr[0:rows, half:] = lax.bitcast_convert_type(pk & jnp.uint32(0xFFFF0000), F32).astype(BF16)
            o_ref[0:rows, :] = jnp.broadcast_to(bd_ref[...], (rows, o_ref.shape[1]))
            if rows < bm:
                o_ref[rows:, :] = jnp.zeros((bm - rows, o_ref.shape[1]), F32)

        xb = xb_scr[0:rows, :]
        for s in range(wg_ref.shape[1] // MOE_SUB):
            cols = slice(s * MOE_SUB, (s + 1) * MOE_SUB)
            g = _dot(xb, wg_ref[:, cols].astype(BF16)) + bg[:, cols]
            u = _dot(xb, wu_ref[:, cols].astype(BF16)) + bu[:, cols]
            g = jnp.minimum(g, SWIGLU_LIMIT)
            u = jnp.clip(u, -SWIGLU_LIMIT, SWIGLU_LIMIT)
            act = (u + 1.0) * (g * _sigmoid(SWIGLU_ALPHA * g))
            o_ref[0:rows, :] += _dot(act.astype(BF16), wd_ref[cols, :].astype(BF16))

    active = b < na_ref[0]
    full = nv_ref[b] > bm // 2
    pl.when(jnp.logical_and(active, full))(lambda: step(bm))
    pl.when(jnp.logical_and(active, jnp.logical_not(full)))(lambda: step(bm // 2))

    @pl.when(jnp.logical_and(jnp.logical_not(active), f == 0))
    def _():
        o_ref[...] = jnp.zeros_like(o_ref)


def _moe_blocks(block_e, n_active, n_valid, xs, w_gate_up, b_gate_up, w_down, b_down, bm, tf):
    n_exp, d, two_f = w_gate_up.shape
    n_rows = xs.shape[0]
    d_ff = two_f // 2
    nb = n_rows // bm
    nf = d_ff // tf

    def blk(b, na):
        return jnp.minimum(b, na[0] - 1)

    def fi(b, f, na):
        return jnp.where(b < na[0], f, nf - 1)

    grid_spec = pltpu.PrefetchScalarGridSpec(
        num_scalar_prefetch=3,
        grid=(nb, nf),
        in_specs=[pl.BlockSpec((bm, d // 2), lambda b, f, be, na, nv: (blk(b, na), 0)),
                  pl.BlockSpec((None, d, tf), lambda b, f, be, na, nv: (be[blk(b, na)], 0, fi(b, f, na))),
                  pl.BlockSpec((None, d, tf), lambda b, f, be, na, nv: (be[blk(b, na)], 0, fi(b, f, na) + nf)),
                  pl.BlockSpec((None, tf, d), lambda b, f, be, na, nv: (be[blk(b, na)], fi(b, f, na), 0)),
                  pl.BlockSpec((None, 2 * nf, tf), lambda b, f, be, na, nv: (be[blk(b, na)], 0, 0)),
                  pl.BlockSpec((None, 1, d), lambda b, f, be, na, nv: (be[blk(b, na)], 0, 0))],
        out_specs=pl.BlockSpec((bm, d), lambda b, f, be, na, nv: (b, 0)),
        scratch_shapes=[pltpu.VMEM((bm, d), BF16)],
    )
    return pl.pallas_call(
        _moe_kernel,
        grid_spec=grid_spec,
        out_shape=jax.ShapeDtypeStruct((n_rows, d), F32),
        compiler_params=_cparams(("arbitrary", "arbitrary"), MOE_VMEM_LIMIT),
        name="moe_experts",
    )(block_e, n_active, n_valid, xs, w_gate_up, w_gate_up, w_down,
      b_gate_up.reshape(n_exp, 2 * nf, tf), b_down.reshape(n_exp, 1, d))


def _combine_kernel(cnt_ref, glb_ref, x1_ref, gt_ref, te_ref, lr_ref, os_ref, ya_ref, yb_ref, buf, loc_scr, sem, *,
                    n_first):
    tile = pl.program_id(0)

    @pl.when(tile == 0)
    def _():
        buf[...] = jnp.zeros_like(buf)

    _run_layout(cnt_ref, glb_ref, tile, loc_scr)

    def make_copy(loc, glb, size):
        return pltpu.make_async_copy(os_ref.at[pl.ds(glb, size)], buf.at[pl.ds(loc, size)], sem)

    _run_copies(cnt_ref, glb_ref, loc_scr, tile, make_copy, lambda cp: cp.start())
    pos, slot_iota = _positions(te_ref[...], lr_ref[...], loc_scr, buf.shape[0])
    gt = gt_ref[...]
    wsel = jnp.zeros(slot_iota.shape, F32)
    for k in range(TOP_K):
        wsel = wsel + jnp.where(slot_iota == pos[k], gt[:, k:k + 1], 0.0)
    _run_copies(cnt_ref, glb_ref, loc_scr, tile, make_copy, lambda cp: cp.wait())
    wh, wl = _split2(wsel)
    oh, ol = _split2(buf[...])
    y = x1_ref[...] + (_dot(wh, oh) + _dot(wl, oh) + _dot(wh, ol))

    @pl.when(tile < n_first)
    def _():
        ya_ref[...] = y

    @pl.when(tile >= n_first)
    def _():
        yb_ref[...] = y


def _combine(cnt, glb, x1, gates, te, lr, o_sorted, tm, t_first):
    t, d = x1.shape
    n_first = t_first // tm
    grid_spec = pltpu.PrefetchScalarGridSpec(
        num_scalar_prefetch=2,
        grid=(t // tm,),
        in_specs=[pl.BlockSpec((tm, d), lambda i, *_: (i, 0)),
                  pl.BlockSpec((tm, LANES), lambda i, *_: (i, 0)),
                  pl.BlockSpec((tm, LANES), lambda i, *_: (i, 0)),
                  pl.BlockSpec((tm, LANES), lambda i, *_: (i, 0)),
                  pl.BlockSpec(memory_space=pl.ANY)],
        out_specs=list(_two_group_specs((tm, d), n_first)),
        scratch_shapes=[pltpu.VMEM((_local_rows(tm), d), F32), pltpu.SMEM((N_EXPERTS,), I32),
                        pltpu.SemaphoreType.DMA],
    )
    return pl.pallas_call(
        functools.partial(_combine_kernel, n_first=n_first),
        grid_spec=grid_spec,
        out_shape=[jax.ShapeDtypeStruct((t_first, d), F32), jax.ShapeDtypeStruct((t - t_first, d), F32)],
        compiler_params=_cparams(("arbitrary",)),
        name="moe_combine",
    )(cnt, glb, x1, gates, te, lr, o_sorted)


def _pick(n, pref):
    while n % pref:
        pref //= 2
    return pref


def _forward(x_prompt, x_sample, cache_k, cache_v, state_rec, norm_mix_w, w_in, q_norm_w, k_norm_w,
             attn_sinks, rec_lb_logits, rec_norm_w, w_attn_branch, w_rec_branch, w_out, norm_ffn_w,
             router_w, router_b, w_gate_up, b_gate_up, w_down, b_down, moe_bm=1024, moe_tf=512):
    bp, sp, d = x_prompt.shape
    bs, ss, _ = x_sample.shape
    assert ss == CHUNK and sp % CHUNK == 0 and sp >= WINDOW
    tp, ts = bp * sp, bs * ss
    t = tp + ts
    xp2 = x_prompt.reshape(tp, d)
    xs2 = x_sample.reshape(ts, d)

    w = w_in[0]
    o_q, o_k, o_v, o_r = 0, Q_W, Q_W + KV_W, Q_W + 2 * KV_W
    o_g = o_r + 4 * REC_W
    w_perm = jnp.concatenate([w[:, o_g:], w[:, o_q:o_k], w[:, o_r:o_g], w[:, o_k:o_r]], axis=1).astype(BF16)
    c_aq = 2 * d
    c_rec = c_aq + Q_W
    c_k = c_rec + 4 * REC_W
    c_v = c_k + KV_W

    proj = _in_proj(xp2, xs2, norm_mix_w, w_perm, _pick(ts, 512), w_perm.shape[1] // 4)

    pos = jnp.concatenate([jnp.tile(jnp.arange(sp, dtype=I32), bp),
                           jnp.tile(PAST_LEN + jnp.arange(ss, dtype=I32), bs)]).astype(F32)
    half = HEAD_DIM // 2
    inv_freq = ROPE_THETA ** (-jnp.arange(half, dtype=F32) / half)
    ang = pos[:, None] * inv_freq[None, :]
    cos = jnp.tile(jnp.cos(ang), (1, LANES // half))
    sgn = jnp.tile(jnp.concatenate([-jnp.ones((half,), F32), jnp.ones((half,), F32)]), LANES // HEAD_DIM)
    sin = jnp.tile(jnp.sin(ang), (1, LANES // half)) * sgn[None, :]
    qw = jnp.tile(q_norm_w[0], LANES // HEAD_DIM)[None, :]
    kw = jnp.tile(k_norm_w[0], LANES // HEAD_DIM)[None, :]
    qr, kr = _qk_prep(proj, cos, sin, qw, kw, c_aq // Q_W, c_k // KV_W, _pick(ts, 512))

    sinks = attn_sinks[0]
    npc = sp // CHUNK
    vcol = c_v // KV_W

    assert npc % 2 == 0
    npp = npc // 2

    def prow(back):
        return lambda b, c: b * npp + jnp.maximum(c - back, 0)

    oa_p = _attention(sinks, qr, [(kr, 0)] * 2, [(proj, vcol)] * 2, (bp, npp),
                      prow(0), prow(0), tp, [prow(1), prow(0)], 2 * CHUNK, 2, True)
    ck = cache_k[0].reshape(bs * WINDOW, KV_W)
    cv = cache_v[0].reshape(bs * WINDOW, KV_W)
    srow = tp // CHUNK
    oa_s = _attention(sinks, qr, [(ck, 0), (ck, 0), (kr, 0)], [(cv, 0), (cv, 0), (proj, vcol)], (bs,),
                      lambda b: srow + b, lambda b: b, ts,
                      [lambda b: 2 * b, lambda b: 2 * b + 1, lambda b: srow + b], CHUNK, 1, False)

    lb = jax.nn.softmax(rec_lb_logits.astype(F32), axis=0)[0][None, :]
    rec_cols = tuple(c_rec + i * REC_W for i in range(4))
    nw_rec = rec_norm_w[0][None, :]
    tb = _pick(sp, 512)
    or_p, sfin_p = _hgrn2(proj, lb, nw_rec, jnp.zeros((bp, N_REC_HEADS, REC_D, REC_D), F32),
                          0, bp, sp, tb, rec_cols, tp)
    or_s, sfin_s = _hgrn2(proj, lb, nw_rec, state_rec[0], tp // CHUNK, bs, ss, CHUNK, rec_cols, ts)

    rw = jnp.pad(router_w[0], ((0, 0), (0, LANES - N_EXPERTS)))
    rwh = rw.astype(BF16)
    rwl = (rw - rwh.astype(F32)).astype(BF16)
    rb = jnp.concatenate([router_b[0].astype(F32), jnp.full((LANES - N_EXPERTS,), -jnp.inf, F32)])[None, :]
    tm = _pick(ts, 256)
    x1, h2, gt, te, lr, tcnt, tpre = _merge(xp2, xs2, oa_p, oa_s, or_p, or_s, proj, 0, 1,
                                            w_attn_branch[0].astype(BF16),
                                            w_rec_branch[0].astype(BF16), w_out[0].astype(BF16),
                                            norm_ffn_w, rwh, rwl, rb, tm)

    tcnt = tcnt[:, 0, :N_EXPERTS]
    counts = jnp.sum(tcnt, axis=0)
    padded = (counts + moe_bm - 1) // moe_bm * moe_bm
    pad_end = jnp.cumsum(padded)
    pad_start = pad_end - padded
    nb = (t * TOP_K) // moe_bm + N_EXPERTS
    block_start = jnp.arange(nb, dtype=I32) * moe_bm
    block_e = jnp.minimum(jnp.sum((pad_end[None, :] <= block_start[:, None]).astype(I32), axis=1), N_EXPERTS - 1)
    n_active = (pad_end[-1:] // moe_bm).astype(I32)
    n_valid = jnp.clip((pad_start + counts)[block_e] - block_start, 0, moe_bm).astype(I32)
    run_cnt = tcnt.reshape(-1)
    run_glb = (pad_start[None, :] + tpre[:, 0, :N_EXPERTS]).reshape(-1)

    pad_n = jnp.concatenate([padded - counts, (nb * moe_bm - pad_end[-1:]) // (moe_bm // 2)]).astype(I32)
    pad_lo = jnp.concatenate([pad_start + counts, pad_end[-1:]]).astype(I32)
    xs = _dispatch(run_cnt, run_glb, pad_n, pad_lo, h2, te, lr, nb * moe_bm, tm, moe_bm)
    o_sorted = _moe_blocks(block_e, n_active, n_valid, xs, w_gate_up[0], b_gate_up[0], w_down[0], b_down[0],
                           moe_bm, moe_tf)
    y_p, y_s = _combine(run_cnt, run_glb, x1, gt, te, lr, o_sorted, tm, tp)
    y_p = y_p.reshape(bp, sp, d)
    y_s = y_s.reshape(bs, ss, d)
    v_new = proj[:, c_v:c_v + KV_W]
    kp = kr[:tp].reshape(bp, sp, N_KV_HEADS, HEAD_DIM)[:, -WINDOW:]
    vp = v_new[:tp].reshape(bp, sp, N_KV_HEADS, HEAD_DIM)[:, -WINDOW:]
    ks_new = kr[tp:].reshape(bs, ss, N_KV_HEADS, HEAD_DIM)
    vs_new = v_new[tp:].reshape(bs, ss, N_KV_HEADS, HEAD_DIM)
    ks = jnp.concatenate([cache_k[0], ks_new], axis=1)[:, -WINDOW:]
    vs = jnp.concatenate([cache_v[0], vs_new], axis=1)[:, -WINDOW:]
    return (y_p, y_s, kp[None], vp[None], sfin_p[None], ks[None], vs[None], sfin_s[None])


def kernel(x_prompt, x_sample, cache_k, cache_v, state_rec, norm_mix_w, w_in, q_norm_w, k_norm_w, attn_sinks, rec_lb_logits, rec_norm_w, w_attn_branch, w_rec_branch, w_out, norm_ffn_w, router_w, router_b, w_gate_up, b_gate_up, w_down, b_down):
    return _forward(x_prompt, x_sample, cache_k, cache_v, state_rec, norm_mix_w, w_in, q_norm_w, k_norm_w,
                    attn_sinks, rec_lb_logits, rec_norm_w, w_attn_branch, w_rec_branch, w_out, norm_ffn_w,
                    router_w, router_b, w_gate_up, b_gate_up, w_down, b_down)
```

```python
import functools

import jax
import jax.numpy as jnp
from jax import lax
from jax.experimental import pallas as pl
from jax.experimental.pallas import tpu as pltpu

F32 = jnp.float32
BF16 = jnp.bfloat16
I32 = jnp.int32

CHUNK = 64
N_HEADS = 16
N_KV_HEADS = 4
HEAD_DIM = 64
WINDOW = 128
ROPE_THETA = 10000.0
N_REC_HEADS = 8
REC_D = 128
N_EXPERTS = 32
TOP_K = 4
SWIGLU_LIMIT = 7.0
SWIGLU_ALPHA = 1.702
EPS = 1e-6
PAST_LEN = 2048
LANES = 128
SUB = 16

VMEM_LIMIT = 56 * 1024 * 1024

Q_W = N_HEADS * HEAD_DIM
KV_W = N_KV_HEADS * HEAD_DIM
REC_W = N_REC_HEADS * REC_D


def _cparams(sem, vmem_limit=VMEM_LIMIT):
    return pltpu.CompilerParams(dimension_semantics=sem, vmem_limit_bytes=vmem_limit)


def _sigmoid(x):
    return 1.0 / (1.0 + jnp.exp(-x))


def _split2(x):
    hi = x.astype(BF16)
    lo = (x - hi.astype(F32)).astype(BF16)
    return hi, lo


def _dot(a, b):
    return jnp.dot(a, b, preferred_element_type=F32)


def _dot_nt(a, b):
    return lax.dot_general(a, b, (((1,), (1,)), ((), ())), preferred_element_type=F32)


def _two_group_specs(shape, n_first):
    first = pl.BlockSpec(shape, lambda i, *_: (jnp.minimum(i, n_first - 1), 0))
    second = pl.BlockSpec(shape, lambda i, *_: (jnp.maximum(i - n_first, 0), 0))
    return first, second


def _in_proj_kernel(xa_ref, xb_ref, nw_ref, w_ref, o_ref, h_scr, *, n_first):
    def norm(x_ref):
        x = x_ref[...]
        ms = jnp.mean(x * x, axis=-1, keepdims=True)
        h_scr[...] = (x * lax.rsqrt(ms + EPS) * nw_ref[...]).astype(BF16)

    first_col = pl.program_id(1) == 0
    in_first = pl.program_id(0) < n_first
    pl.when(jnp.logical_and(first_col, in_first))(lambda: norm(xa_ref))
    pl.when(jnp.logical_and(first_col, jnp.logical_not(in_first)))(lambda: norm(xb_ref))
    o_ref[...] = _dot(h_scr[...], w_ref[...])


def _in_proj(xa, xb, norm_w, w, tm, tn):
    d = xa.shape[1]
    t = xa.shape[0] + xb.shape[0]
    n = w.shape[1]
    n_first = xa.shape[0] // tm
    return pl.pallas_call(
        functools.partial(_in_proj_kernel, n_first=n_first),
        grid=(t // tm, n // tn),
        in_specs=[*_two_group_specs((tm, d), n_first),
                  pl.BlockSpec((1, d), lambda i, j: (0, 0)),
                  pl.BlockSpec((d, tn), lambda i, j: (0, j))],
        out_specs=pl.BlockSpec((tm, tn), lambda i, j: (i, j)),
        out_shape=jax.ShapeDtypeStruct((t, n), F32),
        scratch_shapes=[pltpu.VMEM((tm, d), BF16)],
        compiler_params=_cparams(("parallel", "arbitrary")),
        name="in_proj",
    )(xa, xb, norm_w, w)


def _norm_rope(x, nw, cos, sin, bd):
    hi, lo = _split2(x * x)
    ss = _dot(hi, bd) + _dot(lo, bd)
    y = x * lax.rsqrt(ss * (1.0 / HEAD_DIM) + EPS) * nw
    lane = lax.broadcasted_iota(I32, y.shape, 1)
    first_half = (lane % HEAD_DIM) < (HEAD_DIM // 2)
    rot = jnp.where(first_half, pltpu.roll(y, LANES - HEAD_DIM // 2, 1), pltpu.roll(y, HEAD_DIM // 2, 1))
    return y * cos + rot * sin


def _qk_prep_kernel(q_ref, k_ref, cos_ref, sin_ref, qw_ref, kw_ref, qo_ref, ko_ref):
    r = lax.broadcasted_iota(I32, (LANES, LANES), 0) // HEAD_DIM
    c = lax.broadcasted_iota(I32, (LANES, LANES), 1) // HEAD_DIM
    bd = jnp.where(r == c, 1.0, 0.0).astype(BF16)
    cos = cos_ref[...]
    sin = sin_ref[...]
    scale = HEAD_DIM ** -0.5
    for g in range(Q_W // LANES):
        sl = slice(g * LANES, (g + 1) * LANES)
        qo_ref[:, sl] = (_norm_rope(q_ref[:, sl], qw_ref[...], cos, sin, bd) * scale).astype(BF16)
    for g in range(KV_W // LANES):
        sl = slice(g * LANES, (g + 1) * LANES)
        ko_ref[:, sl] = _norm_rope(k_ref[:, sl], kw_ref[...], cos, sin, bd)


def _qk_prep(proj, cos, sin, qw, kw, q_blk, k_blk, tr):
    t = proj.shape[0]
    return pl.pallas_call(
        _qk_prep_kernel,
        grid=(t // tr,),
        in_specs=[pl.BlockSpec((tr, Q_W), lambda i: (i, q_blk)),
                  pl.BlockSpec((tr, KV_W), lambda i: (i, k_blk)),
                  pl.BlockSpec((tr, LANES), lambda i: (i, 0)),
                  pl.BlockSpec((tr, LANES), lambda i: (i, 0)),
                  pl.BlockSpec((1, LANES), lambda i: (0, 0)),
                  pl.BlockSpec((1, LANES), lambda i: (0, 0))],
        out_specs=[pl.BlockSpec((tr, Q_W), lambda i: (i, 0)),
                   pl.BlockSpec((tr, KV_W), lambda i: (i, 0))],
        out_shape=[jax.ShapeDtypeStruct((t, Q_W), BF16), jax.ShapeDtypeStruct((t, KV_W), F32)],
        compiler_params=_cparams(("parallel",)),
        name="qk_prep",
    )(proj, proj, cos, sin, qw, kw)


def _attn_kernel(sink_ref, q_ref, *refs, masked, n_kv, n_sub):
    k_all = jnp.concatenate([r[...] for r in refs[:n_kv]], axis=0)
    v_all = jnp.concatenate([r[...] for r in refs[n_kv:2 * n_kv]], axis=0)
    o_ref = refs[2 * n_kv]
    n_keys = (n_sub + 2) * CHUNK
    n_rows = n_sub * 2 * CHUNK
    lane = lax.broadcasted_iota(I32, (n_keys, LANES), 1)
    low = lane < HEAD_DIM
    row = lax.broadcasted_iota(I32, (n_rows, 1), 0)
    second_pair = (row // CHUNK) % 2 == 1
    rel = (lax.broadcasted_iota(I32, (n_rows, n_keys), 1) // CHUNK
           - lax.broadcasted_iota(I32, (n_rows, n_keys), 0) // (2 * CHUNK))
    valid = jnp.logical_and(rel >= 0, rel <= 2)
    if masked:
        key_chunk = lax.broadcasted_iota(I32, (n_rows, n_keys), 1) // CHUNK
        valid = jnp.logical_and(valid, (key_chunk + n_sub * pl.program_id(1)) >= 2)
    for j in range(N_KV_HEADS):
        pair = slice((j // 2) * LANES, (j // 2 + 1) * LANES)
        kp = k_all[:, pair]
        vp = v_all[:, pair]
        kr = pltpu.roll(kp, HEAD_DIM, 1)
        vr = pltpu.roll(vp, HEAD_DIM, 1)
        if j % 2 == 0:
            k_low, k_high, v_low, v_high = kp, kr, vp, vr
        else:
            k_low, k_high, v_low, v_high = kr, kp, vr, vp
        q2 = jnp.concatenate([q_ref[u * CHUNK:(u + 1) * CHUNK, (2 * j + pp) * LANES:(2 * j + pp + 1) * LANES]
                              for u in range(n_sub) for pp in range(2)], axis=0)
        o = jnp.zeros((n_rows, LANES), F32)
        for half, (kk, vv) in enumerate(((k_low, v_low), (k_high, v_high))):
            keep = low if half == 0 else jnp.logical_not(low)
            kz = jnp.where(keep, kk, 0.0).astype(BF16)
            vz = jnp.where(keep, vv, 0.0).astype(BF16)
            s = jnp.where(valid, _dot_nt(q2, kz), -jnp.inf)
            sink = jnp.where(second_pair, sink_ref[4 * j + 2 + half], sink_ref[4 * j + half])
            m = jnp.maximum(jnp.max(s, axis=-1, keepdims=True), sink)
            p = jnp.exp(s - m)
            p = p / (jnp.sum(p, axis=-1, keepdims=True) + jnp.exp(sink - m))
            o = o + _dot(p.astype(BF16), vz)
        for u in range(n_sub):
            for pp in range(2):
                r0 = (2 * u + pp) * CHUNK
                o_ref[u * CHUNK:(u + 1) * CHUNK, (2 * j + pp) * LANES:(2 * j + pp + 1) * LANES] = (
                    o[r0:r0 + CHUNK].astype(BF16))


def _attention(sinks, qr, k_srcs, v_srcs, grid, q_map, o_map, n_rows, kv_maps, kv_rows, n_sub, masked):
    assert kv_rows * len(k_srcs) == (n_sub + 2) * CHUNK

    def spec(col, fn):
        return pl.BlockSpec((kv_rows, KV_W), lambda *g: (fn(*g), col))

    in_specs = [pl.BlockSpec(memory_space=pltpu.SMEM),
                pl.BlockSpec((n_sub * CHUNK, Q_W), lambda *g: (q_map(*g), 0))]
    in_specs += [spec(col, fn) for (_, col), fn in zip(k_srcs, kv_maps)]
    in_specs += [spec(col, fn) for (_, col), fn in zip(v_srcs, kv_maps)]
    return pl.pallas_call(
        functools.partial(_attn_kernel, masked=masked, n_kv=len(k_srcs), n_sub=n_sub),
        grid=grid,
        in_specs=in_specs,
        out_specs=pl.BlockSpec((n_sub * CHUNK, Q_W), lambda *g: (o_map(*g), 0)),
        out_shape=jax.ShapeDtypeStruct((n_rows, Q_W), BF16),
        compiler_params=_cparams(("parallel",) * len(grid)),
        name="attn_prompt" if masked else "attn_sample",
    )(sinks, qr, *[a for a, _ in k_srcs], *[a for a, _ in v_srcs])


HEADS_PER_STEP = 8


def _hgrn2_chunk(r0, sl, hh, rq_ref, rf_ref, rv_ref, rg_ref, lb_ref, nw_ref, o_ref,
                 st_scr, q_scr, k_scr, v_scr, c_scr, tri, ones, rsub):
    n_sub = CHUNK // SUB
    rq = rq_ref[pl.ds(r0, CHUNK), sl]
    q = rq * _sigmoid(rq)
    k = (1.0 - lb_ref[:, sl]) * _sigmoid(-rf_ref[pl.ds(r0, CHUNK), sl])
    g = jnp.log1p(-k)
    v = rv_ref[pl.ds(r0, CHUNK), sl]
    g1 = g.astype(BF16)
    e1 = g - g1.astype(F32)
    g2 = e1.astype(BF16)
    g3 = (e1 - g2.astype(F32)).astype(BF16)
    cum = _dot(tri, g1) + _dot(tri, g2) + _dot(tri, g3)
    q_scr[hh] = q
    k_scr[hh] = k
    v_scr[hh] = v
    c_scr[hh] = cum
    st = st_scr[hh]
    inter = _dot_nt((q * jnp.exp(cum)).astype(BF16), st.astype(BF16))
    outs = []
    for i in range(n_sub):
        lo = i * SUB
        q_i = q_scr[hh, lo:lo + SUB, :]
        c_i = c_scr[hh, lo:lo + SUB, :]
        acc = inter[lo:lo + SUB, :]
        if i > 0:
            ref = c_scr[hh, lo - 1:lo, :]
            qd = (q_i * jnp.exp(c_i - ref)).astype(BF16)
            kd = (k_scr[hh, 0:lo, :] * jnp.exp(ref - c_scr[hh, 0:lo, :])).astype(BF16)
            s = _dot_nt(qd, kd)
            acc = acc + _dot(s.astype(BF16), v_scr[hh, 0:lo, :].astype(BF16))
        parts = []
        for s_ in range(SUB):
            arg = jnp.where(rsub >= s_, c_i - c_scr[hh, lo + s_:lo + s_ + 1, :], -jnp.inf)
            parts.append(q_i * k_scr[hh, lo + s_:lo + s_ + 1, :] * jnp.exp(arg))
        dsum = _dot(jnp.concatenate(parts, axis=0).astype(BF16), ones)
        for s_ in range(SUB):
            acc = acc + dsum[s_ * SUB:(s_ + 1) * SUB, :] * v_scr[hh, lo + s_:lo + s_ + 1, :]
        outs.append(acc)
    o = jnp.concatenate(outs, axis=0)
    last = c_scr[hh, CHUNK - 1:CHUNK, :]
    kdec = (k * jnp.exp(last - cum)).astype(BF16)
    st_scr[hh] = st * jnp.exp(last) + _dot(v.T.astype(BF16), kdec)
    ms = jnp.mean(o * o, axis=-1, keepdims=True)
    rg = rg_ref[pl.ds(r0, CHUNK), sl]
    o_ref[pl.ds(r0, CHUNK), sl] = (o * lax.rsqrt(ms + EPS) * nw_ref[...] * (rg * _sigmoid(rg))).astype(BF16)


def _hgrn2_kernel(rq_ref, rf_ref, rv_ref, rg_ref, lb_ref, nw_ref, s0_ref, o_ref, sfin_ref,
                  st_scr, q_scr, k_scr, v_scr, c_scr, *, n_chunks):
    t = pl.program_id(2)

    @pl.when(t == 0)
    def _():
        for hh in range(HEADS_PER_STEP):
            st_scr[hh] = s0_ref[0, hh].T

    r64 = lax.broadcasted_iota(I32, (CHUNK, CHUNK), 0)
    c64 = lax.broadcasted_iota(I32, (CHUNK, CHUNK), 1)
    tri = jnp.where(r64 >= c64, 1.0, 0.0).astype(BF16)
    ones = jnp.ones((LANES, LANES), BF16)
    rsub = lax.broadcasted_iota(I32, (SUB, LANES), 0)

    def chunk(ci, carry):
        r0 = pl.multiple_of(ci * CHUNK, CHUNK)
        for hh in range(HEADS_PER_STEP):
            sl = slice(hh * REC_D, (hh + 1) * REC_D)
            _hgrn2_chunk(r0, sl, hh, rq_ref, rf_ref, rv_ref, rg_ref, lb_ref, nw_ref, o_ref,
                         st_scr, q_scr, k_scr, v_scr, c_scr, tri, ones, rsub)
        return carry

    lax.fori_loop(0, n_chunks, chunk, 0)

    @pl.when(t == pl.num_programs(2) - 1)
    def _():
        for hh in range(HEADS_PER_STEP):
            sfin_ref[0, hh] = st_scr[hh].T


def _hgrn2(proj, lb, norm_w, s0, row_blk0, n_seq, seq_len, tb, col_blks, out_rows):
    nt = seq_len // tb
    hb = HEADS_PER_STEP
    wb = hb * REC_D
    cq, cf, cv, cg = col_blks

    def in_spec(c0):
        return pl.BlockSpec((tb, wb), lambda b, h, t: (row_blk0 + b * nt + t, c0 // wb + h))

    return pl.pallas_call(
        functools.partial(_hgrn2_kernel, n_chunks=tb // CHUNK),
        grid=(n_seq, N_REC_HEADS // hb, nt),
        in_specs=[in_spec(cq), in_spec(cf), in_spec(cv), in_spec(cg),
                  pl.BlockSpec((1, wb), lambda b, h, t: (0, h)),
                  pl.BlockSpec((1, REC_D), lambda b, h, t: (0, 0)),
                  pl.BlockSpec((1, hb, REC_D, REC_D), lambda b, h, t: (b, h, 0, 0))],
        out_specs=[pl.BlockSpec((tb, wb), lambda b, h, t: (b * nt + t, h)),
                   pl.BlockSpec((1, hb, REC_D, REC_D), lambda b, h, t: (b, h, 0, 0))],
        out_shape=[jax.ShapeDtypeStruct((out_rows, REC_W), BF16),
                   jax.ShapeDtypeStruct((n_seq, N_REC_HEADS, REC_D, REC_D), F32)],
        scratch_shapes=[pltpu.VMEM((hb, REC_D, REC_D), F32)] + [pltpu.VMEM((hb, CHUNK, REC_D), F32)] * 4,
        compiler_params=_cparams(("parallel", "parallel", "arbitrary")),
        name="hgrn2",
    )(proj, proj, proj, proj, lb, norm_w, s0)


def _pack_bf16_pairs(h):
    d = h.shape[1]
    bits = lax.bitcast_convert_type(h, jnp.uint32)
    return (bits[:, d // 2:] & jnp.uint32(0xFFFF0000)) | (bits[:, :d // 2] >> 16)


def _merge_kernel(xa_ref, xb_ref, oaa_ref, oab_ref, ora_ref, orb_ref, ga_ref, gr_ref, wa_ref, wr_ref, wo_ref,
                  nf_ref, rwh_ref, rwl_ref, rb_ref,
                  x1_ref, h2_ref, gt_ref, te_ref, lr_ref, tcnt_ref, tpre_ref, carry_scr, *, n_first):
    i = pl.program_id(0)

    @pl.when(i == 0)
    def _():
        carry_scr[...] = jnp.zeros_like(carry_scr)

    in_first = i < n_first
    a = _dot(jnp.where(in_first, oaa_ref[...], oab_ref[...]), wa_ref[...])
    r = _dot(jnp.where(in_first, ora_ref[...], orb_ref[...]), wr_ref[...])
    merged = _sigmoid(ga_ref[...]) * a + _sigmoid(gr_ref[...]) * r
    x1 = jnp.where(in_first, xa_ref[...], xb_ref[...]) + _dot(merged.astype(BF16), wo_ref[...])
    x1_ref[...] = x1
    ms = jnp.mean(x1 * x1, axis=-1, keepdims=True)
    h2 = x1 * lax.rsqrt(ms + EPS) * nf_ref[...]
    tm = h2.shape[0]
    hh, hl = _split2(h2)
    logits = _dot(hh, rwh_ref[...]) + _dot(hl, rwh_ref[...]) + _dot(hh, rwl_ref[...]) + rb_ref[...]
    lane = lax.broadcasted_iota(I32, (tm, LANES), 1)
    lane_f = lane.astype(F32)
    work = logits
    h2_ref[...] = hh
    te = jnp.zeros((tm, LANES), I32)
    tv = jnp.zeros((tm, LANES), F32)
    onehot = jnp.zeros((tm, LANES), F32)
    sels = []
    for k in range(TOP_K):
        m = jnp.max(work, axis=-1, keepdims=True)
        idx = jnp.min(jnp.where(work == m, lane_f, float(LANES)), axis=-1, keepdims=True).astype(I32)
        sel = lane == idx
        work = jnp.where(sel, -jnp.inf, work)
        onehot = onehot + jnp.where(sel, 1.0, 0.0)
        te = jnp.where(lane == k, idx, te)
        tv = jnp.where(lane == k, m, tv)
        sels.append(sel)
    ex = jnp.where(lane < TOP_K, jnp.exp(tv - tv[:, 0:1]), 0.0)
    gt_ref[...] = ex / jnp.sum(ex, axis=-1, keepdims=True)
    te_ref[...] = te
    rr = lax.broadcasted_iota(I32, (tm, tm), 0)
    cc = lax.broadcasted_iota(I32, (tm, tm), 1)
    strict = jnp.where(rr > cc, 1.0, 0.0).astype(BF16)
    lrank = _dot(strict, onehot.astype(BF16))
    lr = jnp.zeros((tm, LANES), F32)
    for k in range(TOP_K):
        lr = jnp.where(lane == k, jnp.sum(jnp.where(sels[k], lrank, 0.0), axis=-1, keepdims=True), lr)
    lr_ref[...] = lr.astype(I32)
    tcnt = jnp.sum(onehot, axis=0, keepdims=True)
    tcnt_ref[0] = tcnt.astype(I32)
    tpre_ref[0] = carry_scr[...].astype(I32)
    carry_scr[...] = carry_scr[...] + tcnt


def _merge(xa, xb, oaa, oab, ora, orb, proj, ga_blk, gr_blk, wa, wr, wo, nf, rwh, rwl, rb, tm):
    d = xa.shape[1]
    t = xa.shape[0] + xb.shape[0]
    nt = t // tm
    n_first = xa.shape[0] // tm

    def const(shape):
        return pl.BlockSpec(shape, lambda i: (0,) * len(shape), pipeline_mode=pl.Buffered(1))

    tile_vec = pl.BlockSpec((1, 1, LANES), lambda i: (i, 0, 0))
    return pl.pallas_call(
        functools.partial(_merge_kernel, n_first=n_first),
        grid=(nt,),
        in_specs=[*_two_group_specs((tm, d), n_first),
                  *_two_group_specs((tm, Q_W), n_first),
                  *_two_group_specs((tm, REC_W), n_first),
                  pl.BlockSpec((tm, d), lambda i: (i, ga_blk)),
                  pl.BlockSpec((tm, d), lambda i: (i, gr_blk)),
                  const((Q_W, d)), const((REC_W, d)), const((d, d)), const((1, d)),
                  const((d, LANES)), const((d, LANES)), const((1, LANES))],
        out_specs=[pl.BlockSpec((tm, d), lambda i: (i, 0)),
                   pl.BlockSpec((tm, d), lambda i: (i, 0)),
                   pl.BlockSpec((tm, LANES), lambda i: (i, 0)),
                   pl.BlockSpec((tm, LANES), lambda i: (i, 0)),
                   pl.BlockSpec((tm, LANES), lambda i: (i, 0)),
                   tile_vec, tile_vec],
        out_shape=[jax.ShapeDtypeStruct((t, d), F32), jax.ShapeDtypeStruct((t, d), BF16),
                   jax.ShapeDtypeStruct((t, LANES), F32), jax.ShapeDtypeStruct((t, LANES), I32),
                   jax.ShapeDtypeStruct((t, LANES), I32),
                   jax.ShapeDtypeStruct((nt, 1, LANES), I32), jax.ShapeDtypeStruct((nt, 1, LANES), I32)],
        scratch_shapes=[pltpu.VMEM((1, LANES), F32)],
        compiler_params=_cparams(("arbitrary",)),
        name="merge_router",
    )(xa, xb, oaa, oab, ora, orb, proj, proj, wa, wr, wo, nf, rwh, rwl, rb)


SUBLANES = 8
RUN_BITS = 9


def _local_rows(tm):
    return TOP_K * tm + SUBLANES * N_EXPERTS


def _run_layout(cnt_ref, glb_ref, tile, loc_scr):
    def body(e, nxt):
        j = tile * N_EXPERTS + e
        loc = nxt + ((glb_ref[j] - nxt) & (SUBLANES - 1))
        loc_scr[e] = loc
        return loc + cnt_ref[j]

    lax.fori_loop(0, N_EXPERTS, body, 0)


def _run_copies(cnt_ref, glb_ref, loc_scr, tile, make_copy, action, n_bits=RUN_BITS):
    def per_expert(e, c):
        j = tile * N_EXPERTS + e
        n = cnt_ref[j]
        glb = glb_ref[j]
        loc = loc_scr[e]
        head = jnp.minimum(n, (-glb) & (SUBLANES - 1))
        mid = ((n - head) // SUBLANES) * SUBLANES

        def single(i, c2):
            action(make_copy(loc + i, glb + i, 1))
            return c2

        lax.fori_loop(0, head, single, 0)
        for bit in range(n_bits - 1, 2, -1):
            size = 1 << bit

            @pl.when((mid & size) != 0)
            def _():
                done = head + ((mid >> (bit + 1)) << (bit + 1))
                action(make_copy(pl.multiple_of(loc + done, SUBLANES), pl.multiple_of(glb + done, SUBLANES), size))

        lax.fori_loop(head + mid, n, single, 0)
        return c

    lax.fori_loop(0, N_EXPERTS, per_expert, 0)


def _positions(te, lr, loc_scr, n_local):
    tm = te.shape[0]
    lane = lax.broadcasted_iota(I32, (1, LANES), 1)
    lvec = jnp.zeros((1, LANES), I32)
    for e in range(N_EXPERTS):
        lvec = jnp.where(lane == e, loc_scr[e], lvec)
    lvec = lvec.astype(F32)
    lane_t = lax.broadcasted_iota(I32, (tm, LANES), 1)
    pos = []
    for k in range(TOP_K):
        start = jnp.sum(jnp.where(lane_t == te[:, k:k + 1], lvec, 0.0), axis=-1, keepdims=True)
        pos.append(start.astype(I32) + lr[:, k:k + 1])
    return pos, lax.broadcasted_iota(I32, (tm, n_local), 1)


def _dispatch_kernel(cnt_ref, glb_ref, padn_ref, padlo_ref, h_ref, te_ref, lr_ref, xs_ref, srt_scr, loc_scr, sem, *,
                     pad_bits):
    tile = pl.program_id(0)
    _run_layout(cnt_ref, glb_ref, tile, loc_scr)

    @pl.when(tile == 0)
    def _():
        n_zero = 1 << (pad_bits - 1)
        srt_scr[0:n_zero, :] = jnp.zeros((n_zero, srt_scr.shape[1]), srt_scr.dtype)

        def zero_copy(loc, glb, size):
            return pltpu.make_async_copy(srt_scr.at[pl.ds(0, size)], xs_ref.at[pl.ds(glb, size)], sem)

        _run_copies(padn_ref, padlo_ref, loc_scr, 0, zero_copy, lambda cp: cp.start(), pad_bits)
        _run_copies(padn_ref, padlo_ref, loc_scr, 0, zero_copy, lambda cp: cp.wait(), pad_bits)

        def tail(action):
            def body(i, c):
                start = pl.multiple_of(padlo_ref[N_EXPERTS] + i * n_zero, SUBLANES)
                action(zero_copy(0, start, n_zero))
                return c
            lax.fori_loop(0, padn_ref[N_EXPERTS], body, 0)

        tail(lambda cp: cp.start())
        tail(lambda cp: cp.wait())

    pos, slot_iota = _positions(te_ref[...], lr_ref[...], loc_scr, srt_scr.shape[0])
    perm_t = jnp.zeros(slot_iota.shape, F32)
    for k in range(TOP_K):
        perm_t = perm_t + jnp.where(slot_iota == pos[k], 1.0, 0.0)
    rows = lax.dot_general(perm_t.astype(BF16), h_ref[...], (((0,), (0,)), ((), ())), preferred_element_type=F32)
    srt_scr[...] = _pack_bf16_pairs(rows)

    def make_copy(loc, glb, size):
        return pltpu.make_async_copy(srt_scr.at[pl.ds(loc, size)], xs_ref.at[pl.ds(glb, size)], sem)

    _run_copies(cnt_ref, glb_ref, loc_scr, tile, make_copy, lambda cp: cp.start())
    _run_copies(cnt_ref, glb_ref, loc_scr, tile, make_copy, lambda cp: cp.wait())


def _dispatch(cnt, glb, pad_n, pad_lo, h2, te, lr, n_rows, tm, bm):
    t, d = h2.shape
    assert tm < (1 << RUN_BITS)
    pad_bits = (bm - 1).bit_length()
    assert bm == 1 << pad_bits and bm // 2 <= _local_rows(tm)
    grid_spec = pltpu.PrefetchScalarGridSpec(
        num_scalar_prefetch=4,
        grid=(t // tm,),
        in_specs=[pl.BlockSpec((tm, d), lambda i, *_: (i, 0)),
                  pl.BlockSpec((tm, LANES), lambda i, *_: (i, 0)),
                  pl.BlockSpec((tm, LANES), lambda i, *_: (i, 0))],
        out_specs=pl.BlockSpec(memory_space=pl.ANY),
        scratch_shapes=[pltpu.VMEM((_local_rows(tm), d // 2), jnp.uint32), pltpu.SMEM((N_EXPERTS,), I32),
                        pltpu.SemaphoreType.DMA],
    )
    return pl.pallas_call(
        functools.partial(_dispatch_kernel, pad_bits=pad_bits),
        grid_spec=grid_spec,
        out_shape=jax.ShapeDtypeStruct((n_rows, d // 2), jnp.uint32),
        compiler_params=_cparams(("arbitrary",)),
        name="moe_dispatch",
    )(cnt, glb, pad_n, pad_lo, h2, te, lr)


MOE_SUB = 256
MOE_VMEM_LIMIT = 60 * 1024 * 1024


def _moe_kernel(be_ref, na_ref, nv_ref, x_ref, wg_ref, wu_ref, wd_ref, bgu_ref, bd_ref, o_ref, xb_scr):
    b = pl.program_id(0)
    f = pl.program_id(1)
    nf = pl.num_programs(1)
    bm = o_ref.shape[0]
    half = x_ref.shape[1]
    bg = bgu_ref[pl.ds(f, 1), :]
    bu = bgu_ref[pl.ds(nf + f, 1), :]

    def step(rows):
        @pl.when(f == 0)
        def _():
            pk = x_ref[0:rows, :]
            xb_scr[0:rows, :half] = lax.bitcast_convert_type(pk << 16, F32).astype(BF16)
            xb_scr[0:rows, half:] = lax.bitcast_convert_type(pk & jnp.uint32(0xFFFF0000), F32).astype(BF16)
            o_ref[0:rows, :] = jnp.broadcast_to(bd_ref[...], (rows, o_ref.shape[1]))
            if rows < bm:
                o_ref[rows:, :] = jnp.zeros((bm - rows, o_ref.shape[1]), F32)

        xb = xb_scr[0:rows, :]
        for s in range(wg_ref.shape[1] // MOE_SUB):
            cols = slice(s * MOE_SUB, (s + 1) * MOE_SUB)
            g = _dot(xb, wg_ref[:, cols].astype(BF16)) + bg[:, cols]
            u = _dot(xb, wu_ref[:, cols].astype(BF16)) + bu[:, cols]
            g = jnp.minimum(g, SWIGLU_LIMIT)
            u = jnp.clip(u, -SWIGLU_LIMIT, SWIGLU_LIMIT)
            act = (u + 1.0) * (g * _sigmoid(SWIGLU_ALPHA * g))
            o_ref[0:rows, :] += _dot(act.astype(BF16), wd_ref[cols, :].astype(BF16))

    active = b < na_ref[0]
    full = nv_ref[b] > bm // 2
    pl.when(jnp.logical_and(active, full))(lambda: step(bm))
    pl.when(jnp.logical_and(active, jnp.logical_not(full)))(lambda: step(bm // 2))

    @pl.when(jnp.logical_and(jnp.logical_not(active), f == 0))
    def _():
        o_ref[...] = jnp.zeros_like(o_ref)


def _moe_blocks(block_e, n_active, n_valid, xs, w_gate_up, b_gate_up, w_down, b_down, bm, tf):
    n_exp, d, two_f = w_gate_up.shape
    n_rows = xs.shape[0]
    d_ff = two_f // 2
    nb = n_rows // bm
    nf = d_ff // tf

    def blk(b, na):
        return jnp.minimum(b, na[0] - 1)

    def fi(b, f, na):
        return jnp.where(b < na[0], f, nf - 1)

    grid_spec = pltpu.PrefetchScalarGridSpec(
        num_scalar_prefetch=3,
        grid=(nb, nf),
        in_specs=[pl.BlockSpec((bm, d // 2), lambda b, f, be, na, nv: (blk(b, na), 0)),
                  pl.BlockSpec((None, d, tf), lambda b, f, be, na, nv: (be[blk(b, na)], 0, fi(b, f, na))),
                  pl.BlockSpec((None, d, tf), lambda b, f, be, na, nv: (be[blk(b, na)], 0, fi(b, f, na) + nf)),
                  pl.BlockSpec((None, tf, d), lambda b, f, be, na, nv: (be[blk(b, na)], fi(b, f, na), 0)),
                  pl.BlockSpec((None, 2 * nf, tf), lambda b, f, be, na, nv: (be[blk(b, na)], 0, 0)),
                  pl.BlockSpec((None, 1, d), lambda b, f, be, na, nv: (be[blk(b, na)], 0, 0))],
        out_specs=pl.BlockSpec((bm, d), lambda b, f, be, na, nv: (b, 0)),
        scratch_shapes=[pltpu.VMEM((bm, d), BF16)],
    )
    return pl.pallas_call(
        _moe_kernel,
        grid_spec=grid_spec,
        out_shape=jax.ShapeDtypeStruct((n_rows, d), F32),
        compiler_params=_cparams(("arbitrary", "arbitrary"), MOE_VMEM_LIMIT),
        name="moe_experts",
    )(block_e, n_active, n_valid, xs, w_gate_up, w_gate_up, w_down,
      b_gate_up.reshape(n_exp, 2 * nf, tf), b_down.reshape(n_exp, 1, d))


def _combine_kernel(cnt_ref, glb_ref, x1_ref, gt_ref, te_ref, lr_ref, os_ref, ya_ref, yb_ref, buf, loc_scr, sem, *,
                    n_first):
    tile = pl.program_id(0)

    @pl.when(tile == 0)
    def _():
        buf[...] = jnp.zeros_like(buf)

    _run_layout(cnt_ref, glb_ref, tile, loc_scr)

    def make_copy(loc, glb, size):
        return pltpu.make_async_copy(os_ref.at[pl.ds(glb, size)], buf.at[pl.ds(loc, size)], sem)

    _run_copies(cnt_ref, glb_ref, loc_scr, tile, make_copy, lambda cp: cp.start())
    pos, slot_iota = _positions(te_ref[...], lr_ref[...], loc_scr, buf.shape[0])
    gt = gt_ref[...]
    wsel = jnp.zeros(slot_iota.shape, F32)
    for k in range(TOP_K):
        wsel = wsel + jnp.where(slot_iota == pos[k], gt[:, k:k + 1], 0.0)
    _run_copies(cnt_ref, glb_ref, loc_scr, tile, make_copy, lambda cp: cp.wait())
    wh, wl = _split2(wsel)
    oh, ol = _split2(buf[...])
    y = x1_ref[...] + (_dot(wh, oh) + _dot(wl, oh) + _dot(wh, ol))

    @pl.when(tile < n_first)
    def _():
        ya_ref[...] = y

    @pl.when(tile >= n_first)
    def _():
        yb_ref[...] = y


def _combine(cnt, glb, x1, gates, te, lr, o_sorted, tm, t_first):
    t, d = x1.shape
    n_first = t_first // tm
    grid_spec = pltpu.PrefetchScalarGridSpec(
        num_scalar_prefetch=2,
        grid=(t // tm,),
        in_specs=[pl.BlockSpec((tm, d), lambda i, *_: (i, 0)),
                  pl.BlockSpec((tm, LANES), lambda i, *_: (i, 0)),
                  pl.BlockSpec((tm, LANES), lambda i, *_: (i, 0)),
                  pl.BlockSpec((tm, LANES), lambda i, *_: (i, 0)),
                  pl.BlockSpec(memory_space=pl.ANY)],
        out_specs=list(_two_group_specs((tm, d), n_first)),
        scratch_shapes=[pltpu.VMEM((_local_rows(tm), d), F32), pltpu.SMEM((N_EXPERTS,), I32),
                        pltpu.SemaphoreType.DMA],
    )
    return pl.pallas_call(
        functools.partial(_combine_kernel, n_first=n_first),
        grid_spec=grid_spec,
        out_shape=[jax.ShapeDtypeStruct((t_first, d), F32), jax.ShapeDtypeStruct((t - t_first, d), F32)],
        compiler_params=_cparams(("arbitrary",)),
        name="moe_combine",
    )(cnt, glb, x1, gates, te, lr, o_sorted)


def _pick(n, pref):
    while n % pref:
        pref //= 2
    return pref


def _forward(x_prompt, x_sample, cache_k, cache_v, state_rec, norm_mix_w, w_in, q_norm_w, k_norm_w,
             attn_sinks, rec_lb_logits, rec_norm_w, w_attn_branch, w_rec_branch, w_out, norm_ffn_w,
             router_w, router_b, w_gate_up, b_gate_up, w_down, b_down, moe_bm=1024, moe_tf=512):
    bp, sp, d = x_prompt.shape
    bs, ss, _ = x_sample.shape
    assert ss == CHUNK and sp % CHUNK == 0 and sp >= WINDOW
    tp, ts = bp * sp, bs * ss
    t = tp + ts
    xp2 = x_prompt.reshape(tp, d)
    xs2 = x_sample.reshape(ts, d)

    w = w_in[0]
    o_q, o_k, o_v, o_r = 0, Q_W, Q_W + KV_W, Q_W + 2 * KV_W
    o_g = o_r + 4 * REC_W
    w_perm = jnp.concatenate([w[:, o_g:], w[:, o_q:o_k], w[:, o_r:o_g], w[:, o_k:o_r]], axis=1).astype(BF16)
    c_aq = 2 * d
    c_rec = c_aq + Q_W
    c_k = c_rec + 4 * REC_W
    c_v = c_k + KV_W

    proj = _in_proj(xp2, xs2, norm_mix_w, w_perm, _pick(ts, 512), w_perm.shape[1] // 4)

    pos = jnp.concatenate([jnp.tile(jnp.arange(sp, dtype=I32), bp),
                           jnp.tile(PAST_LEN + jnp.arange(ss, dtype=I32), bs)]).astype(F32)
    half = HEAD_DIM // 2
    inv_freq = ROPE_THETA ** (-jnp.arange(half, dtype=F32) / half)
    ang = pos[:, None] * inv_freq[None, :]
    cos = jnp.tile(jnp.cos(ang), (1, LANES // half))
    sgn = jnp.tile(jnp.concatenate([-jnp.ones((half,), F32), jnp.ones((half,), F32)]), LANES // HEAD_DIM)
    sin = jnp.tile(jnp.sin(ang), (1, LANES // half)) * sgn[None, :]
    qw = jnp.tile(q_norm_w[0], LANES // HEAD_DIM)[None, :]
    kw = jnp.tile(k_norm_w[0], LANES // HEAD_DIM)[None, :]
    qr, kr = _qk_prep(proj, cos, sin, qw, kw, c_aq // Q_W, c_k // KV_W, _pick(ts, 512))

    sinks = attn_sinks[0]
    npc = sp // CHUNK
    vcol = c_v // KV_W

    assert npc % 2 == 0
    npp = npc // 2

    def prow(back):
        return lambda b, c: b * npp + jnp.maximum(c - back, 0)

    oa_p = _attention(sinks, qr, [(kr, 0)] * 2, [(proj, vcol)] * 2, (bp, npp),
                      prow(0), prow(0), tp, [prow(1), prow(0)], 2 * CHUNK, 2, True)
    ck = cache_k[0].reshape(bs * WINDOW, KV_W)
    cv = cache_v[0].reshape(bs * WINDOW, KV_W)
    srow = tp // CHUNK
    oa_s = _attention(sinks, qr, [(ck, 0), (ck, 0), (kr, 0)], [(cv, 0), (cv, 0), (proj, vcol)], (bs,),
                      lambda b: srow + b, lambda b: b, ts,
                      [lambda b: 2 * b, lambda b: 2 * b + 1, lambda b: srow + b], CHUNK, 1, False)

    lb = jax.nn.softmax(rec_lb_logits.astype(F32), axis=0)[0][None, :]
    rec_cols = tuple(c_rec + i * REC_W for i in range(4))
    nw_rec = rec_norm_w[0][None, :]
    tb = _pick(sp, 512)
    or_p, sfin_p = _hgrn2(proj, lb, nw_rec, jnp.zeros((bp, N_REC_HEADS, REC_D, REC_D), F32),
                          0, bp, sp, tb, rec_cols, tp)
    or_s, sfin_s = _hgrn2(proj, lb, nw_rec, state_rec[0], tp // CHUNK, bs, ss, CHUNK, rec_cols, ts)

    rw = jnp.pad(router_w[0], ((0, 0), (0, LANES - N_EXPERTS)))
    rwh = rw.astype(BF16)
    rwl = (rw - rwh.astype(F32)).astype(BF16)
    rb = jnp.concatenate([router_b[0].astype(F32), jnp.full((LANES - N_EXPERTS,), -jnp.inf, F32)])[None, :]
    tm = _pick(ts, 256)
    x1, h2, gt, te, lr, tcnt, tpre = _merge(xp2, xs2, oa_p, oa_s, or_p, or_s, proj, 0, 1,
                                            w_attn_branch[0].astype(BF16),
                                            w_rec_branch[0].astype(BF16), w_out[0].astype(BF16),
                                            norm_ffn_w, rwh, rwl, rb, tm)

    tcnt = tcnt[:, 0, :N_EXPERTS]
    counts = jnp.sum(tcnt, axis=0)
    padded = (counts + moe_bm - 1) // moe_bm * moe_bm
    pad_end = jnp.cumsum(padded)
    pad_start = pad_end - padded
    nb = (t * TOP_K) // moe_bm + N_EXPERTS
    block_start = jnp.arange(nb, dtype=I32) * moe_bm
    block_e = jnp.minimum(jnp.sum((pad_end[None, :] <= block_start[:, None]).astype(I32), axis=1), N_EXPERTS - 1)
    n_active = (pad_end[-1:] // moe_bm).astype(I32)
    n_valid = jnp.clip((pad_start + counts)[block_e] - block_start, 0, moe_bm).astype(I32)
    run_cnt = tcnt.reshape(-1)
    run_glb = (pad_start[None, :] + tpre[:, 0, :N_EXPERTS]).reshape(-1)

    pad_n = jnp.concatenate([padded - counts, (nb * moe_bm - pad_end[-1:]) // (moe_bm // 2)]).astype(I32)
    pad_lo = jnp.concatenate([pad_start + counts, pad_end[-1:]]).astype(I32)
    xs = _dispatch(run_cnt, run_glb, pad_n, pad_lo, h2, te, lr, nb * moe_bm, tm, moe_bm)
    o_sorted = _moe_blocks(block_e, n_active, n_valid, xs, w_gate_up[0], b_gate_up[0], w_down[0], b_down[0],
                           moe_bm, moe_tf)
    y_p, y_s = _combine(run_cnt, run_glb, x1, gt, te, lr, o_sorted, tm, tp)
    y_p = y_p.reshape(bp, sp, d)
    y_s = y_s.reshape(bs, ss, d)
    v_new = proj[:, c_v:c_v + KV_W]
    kp = kr[:tp].reshape(bp, sp, N_KV_HEADS, HEAD_DIM)[:, -WINDOW:]
    vp = v_new[:tp].reshape(bp, sp, N_KV_HEADS, HEAD_DIM)[:, -WINDOW:]
    ks_new = kr[tp:].reshape(bs, ss, N_KV_HEADS, HEAD_DIM)
    vs_new = v_new[tp:].reshape(bs, ss, N_KV_HEADS, HEAD_DIM)
    ks = jnp.concatenate([cache_k[0], ks_new], axis=1)[:, -WINDOW:]
    vs = jnp.concatenate([cache_v[0], vs_new], axis=1)[:, -WINDOW:]
    return (y_p, y_s, kp[None], vp[None], sfin_p[None], ks[None], vs[None], sfin_s[None])


def kernel(x_prompt, x_sample, cache_k, cache_v, state_rec, norm_mix_w, w_in, q_norm_w, k_norm_w, attn_sinks, rec_lb_logits, rec_norm_w, w_attn_branch, w_rec_branch, w_out, norm_ffn_w, router_w, router_b, w_gate_up, b_gate_up, w_down, b_down):
    return _forward(x_prompt, x_sample, cache_k, cache_v, state_rec, norm_mix_w, w_in, q_norm_w, k_norm_w,
                    attn_sinks, rec_lb_logits, rec_norm_w, w_attn_branch, w_rec_branch, w_out, norm_ffn_w,
                    router_w, router_b, w_gate_up, b_gate_up, w_down, b_down)
```
